```python
import jax, jax.numpy as jnp
from jax import lax
import numpy as np

D_MODEL = 2048
BATCH = 4
SEQ = 4096
DEPTH = 4

GRID_W = 64
MIX_WIDTH = D_MODEL
CONV_WIDTH = MIX_WIDTH // 2
NA_HEADS = 16
NA_HEAD_DIM = (MIX_WIDTH - CONV_WIDTH) // NA_HEADS
NA_WIDTH = NA_HEADS * NA_HEAD_DIM
CONV_KERNEL = 31
WIN_ROWS_MAX = 8
WIN_COLS = 16
D_FF = 4 * D_MODEL
IN_COLS = 2 * CONV_WIDTH + 3 * NA_WIDTH
RMS_EPS = 1e-6
LN_EPS = 1e-5
NEG_INF = -1e30

kernel_name = "hybrid_conv_natten_encoder"


def rms_norm(x, g):
    xf = x.astype(jnp.float32)
    y = xf * lax.rsqrt(jnp.mean(xf * xf, axis=-1, keepdims=True) + RMS_EPS)
    return (y * g.astype(jnp.float32)).astype(x.dtype)


def layer_norm(x, g, b):
    xf = x.astype(jnp.float32)
    mu = jnp.mean(xf, axis=-1, keepdims=True)
    xc = xf - mu
    var = jnp.mean(xc * xc, axis=-1, keepdims=True)
    y = xc * lax.rsqrt(var + LN_EPS) * g.astype(jnp.float32) + b.astype(jnp.float32)
    return y.astype(x.dtype)


def conformer_conv_group(a, gate, w_dw, b_dw, ln_g, ln_b):
    u = a * jax.nn.sigmoid(gate)
    u = lax.conv_general_dilated(
        u, w_dw[:, None, :].astype(u.dtype),
        window_strides=(1,),
        padding=[(CONV_KERNEL // 2, CONV_KERNEL // 2)],
        dimension_numbers=("NWC", "WIO", "NWC"),
        feature_group_count=CONV_WIDTH,
    ) + b_dw.astype(u.dtype)
    return jax.nn.silu(layer_norm(u, ln_g, ln_b))


def neighbourhood_attention_group(q, k, v, rpb):
    B, T, _ = q.shape
    rows = T // GRID_W
    kr = min(WIN_ROWS_MAX, rows)
    r = np.arange(rows)
    c = np.arange(GRID_W)
    row_start = np.clip(r - kr // 2, 0, rows - kr)
    row_idx = row_start[:, None] + np.arange(kr)[None, :]
    col_start = np.clip(c - WIN_COLS // 2, 0, GRID_W - WIN_COLS)
    col_mask = (c[None, :] >= col_start[:, None]) & (c[None, :] < col_start[:, None] + WIN_COLS)
    dr = row_idx - r[:, None] + (WIN_ROWS_MAX - 1)
    dc = np.clip(c[None, :] - c[:, None], -(WIN_COLS - 1), WIN_COLS - 1) + (WIN_COLS - 1)
    bias = rpb[:, dr[:, None, :, None], dc[None, :, None, :]].astype(jnp.float32)

    def to_grid(t):
        return t.reshape(B, rows, GRID_W, NA_HEADS, NA_HEAD_DIM)

    qg, kg, vg = to_grid(q), to_grid(k), to_grid(v)
    k_rows = kg[:, row_idx]
    v_rows = vg[:, row_idx]
    s = jnp.einsum("brwhd,brkvhd->bhrwkv", qg, k_rows,
                   preferred_element_type=jnp.float32) * (NA_HEAD_DIM ** -0.5)
    s = jnp.where(col_mask[:, None, :], s + bias, NEG_INF)
    p = jax.nn.softmax(s.reshape(B, NA_HEADS, rows, GRID_W, kr * GRID_W), axis=-1)
    p = p.reshape(s.shape).astype(v.dtype)
    o = jnp.einsum("bhrwkv,brkvhd->brwhd", p, v_rows)
    return o.reshape(B, T, NA_WIDTH)


def setup_inputs(seed: int = 0) -> dict:
    key = jax.random.key(seed)
    ks = jax.random.split(key, 16)
    f32 = jnp.float32
    x = jax.random.normal(ks[0], (BATCH, SEQ, D_MODEL), f32)
    w_in = jax.random.normal(ks[1], (DEPTH, D_MODEL, IN_COLS), f32) * D_MODEL ** -0.5
    w_dw = jax.random.normal(ks[2], (DEPTH, CONV_KERNEL, CONV_WIDTH), f32) * CONV_KERNEL ** -0.5
    b_dw = jax.random.normal(ks[3], (DEPTH, CONV_WIDTH), f32) * 0.01
    conv_ln_g = 1.0 + 0.05 * jax.random.normal(ks[4], (DEPTH, CONV_WIDTH), f32)
    conv_ln_b = 0.01 * jax.random.normal(ks[5], (DEPTH, CONV_WIDTH), f32)
    rpb = 0.02 * jax.random.normal(ks[6], (DEPTH, NA_HEADS, 2 * WIN_ROWS_MAX - 1, 2 * WIN_COLS - 1), f32)
    w_out = jax.random.normal(ks[7], (DEPTH, MIX_WIDTH, D_MODEL), f32) * MIX_WIDTH ** -0.5
    w_up = jax.random.normal(ks[8], (DEPTH, D_MODEL, D_FF), f32) * D_MODEL ** -0.5
    w_down = jax.random.normal(ks[9], (DEPTH, D_FF, D_MODEL), f32) * D_FF ** -0.5
    pre_mix_g = 1.0 + 0.05 * jax.random.normal(ks[10], (DEPTH, D_MODEL), f32)
    post_mix_g = 1.0 + 0.05 * jax.random.normal(ks[11], (DEPTH, D_MODEL), f32)
    pre_mlp_g = 1.0 + 0.05 * jax.random.normal(ks[12], (DEPTH, D_MODEL), f32)
    post_mlp_g = 1.0 + 0.05 * jax.random.normal(ks[13], (DEPTH, D_MODEL), f32)
    return {"x": x, "w_in": w_in, "w_dw": w_dw, "b_dw": b_dw, "conv_ln_g": conv_ln_g,
            "conv_ln_b": conv_ln_b, "rpb": rpb, "w_out": w_out, "w_up": w_up, "w_down": w_down,
            "pre_mix_g": pre_mix_g, "post_mix_g": post_mix_g, "pre_mlp_g": pre_mlp_g,
            "post_mlp_g": post_mlp_g}


def reference(x, w_in, w_dw, b_dw, conv_ln_g, conv_ln_b, rpb, w_out, w_up, w_down,
              pre_mix_g, post_mix_g, pre_mlp_g, post_mlp_g):
    splits = [CONV_WIDTH, 2 * CONV_WIDTH, 2 * CONV_WIDTH + NA_WIDTH, 2 * CONV_WIDTH + 2 * NA_WIDTH]
    for l in range(DEPTH):
        h = rms_norm(x, pre_mix_g[l])
        proj = h @ w_in[l]
        a, gate, q, k, v = jnp.split(proj, splits, axis=-1)
        yc = conformer_conv_group(a, gate, w_dw[l], b_dw[l], conv_ln_g[l], conv_ln_b[l])
        ya = neighbourhood_attention_group(q, k, v, rpb[l])
        mix = jnp.concatenate([yc, ya], axis=-1) @ w_out[l]
        x = x + rms_norm(mix, post_mix_g[l])
        h = rms_norm(x, pre_mlp_g[l])
        f = jnp.square(jax.nn.relu(h @ w_up[l])) @ w_down[l]
        x = x + rms_norm(f, post_mlp_g[l])
    return x
```

```python
import functools

import jax
import jax.numpy as jnp
from jax import lax
from jax.experimental import pallas as pl
from jax.experimental.pallas import tpu as pltpu

LANES = 128
GRID_W = 64
NA_HEADS = 16
HEAD_DIM = 64
PAIR_W = 2 * HEAD_DIM
N_PAIRS = NA_HEADS // 2
CONV_K = 31
WIN_ROWS = 8
WIN_COLS = 16
RMS_EPS = 1e-6
LN_EPS = 1e-5
NEG_INF = -1e30

CONV_HALO = 16
CONV_ROWS = 16

V7X_VMEM_BYTES = 64 * 1024 * 1024
VMEM_RESERVE = 6 * 1024 * 1024


def _params(semantics, est_bytes):
    limit = min(int(est_bytes) + VMEM_RESERVE, V7X_VMEM_BYTES - VMEM_RESERVE)
    return pltpu.CompilerParams(dimension_semantics=semantics, vmem_limit_bytes=limit)


def _nbytes(shape, dtype):
    n = 1
    for s in shape:
        n *= s
    return n * jnp.dtype(dtype).itemsize


def _rms(xf, g):
    y = xf * lax.rsqrt(jnp.mean(xf * xf, axis=-1, keepdims=True) + RMS_EPS)
    return y * g


def _prenorm_kernel(x_ref, g_ref, h_ref):
    h_ref[...] = _rms(x_ref[...], g_ref[...]).astype(h_ref.dtype)


def _prenorm(x2, g, tm=512):
    m, d = x2.shape
    est = 2 * (_nbytes((tm, d), jnp.float32) + _nbytes((tm, d), jnp.bfloat16))
    return pl.pallas_call(
        _prenorm_kernel,
        grid=(m // tm,),
        in_specs=[pl.BlockSpec((tm, d), lambda i: (i, 0)),
                  pl.BlockSpec((1, d), lambda i: (0, 0))],
        out_specs=pl.BlockSpec((tm, d), lambda i: (i, 0)),
        out_shape=jax.ShapeDtypeStruct((m, d), jnp.bfloat16),
        compiler_params=_params(("parallel",), est),
        name="prenorm",
    )(x2, g)


def _matmul_kernel(h_ref, w_ref, o_ref):
    o_ref[...] = jnp.dot(h_ref[...], w_ref[...],
                         preferred_element_type=jnp.float32).astype(o_ref.dtype)


def _matmul(h, w, out_dtype, tm=1024, tn=1024, name="inproj"):
    m, k = h.shape
    n = w.shape[1]
    est = 2 * (_nbytes((tm, k), h.dtype) + _nbytes((k, tn), w.dtype) + _nbytes((tm, tn), out_dtype))
    est += _nbytes((tm, tn), jnp.float32)
    return pl.pallas_call(
        _matmul_kernel,
        grid=(m // tm, n // tn),
        in_specs=[pl.BlockSpec((tm, k), lambda i, j: (i, 0)),
                  pl.BlockSpec((k, tn), lambda i, j: (0, j))],
        out_specs=pl.BlockSpec((tm, tn), lambda i, j: (i, j)),
        out_shape=jax.ShapeDtypeStruct((m, n), out_dtype),
        compiler_params=_params(("parallel", "arbitrary"), est),
        name=name,
    )(h, w)


def _conv_kernel(a_ref, g_ref, ap_ref, gp_ref, an_ref, gn_ref, w_ref, b_ref, lg_ref, lb_ref,
                 o_ref, ext_ref, *, tt, tiles_per_seq):
    ti = lax.rem(pl.program_id(0), tiles_per_seq)
    groups = ext_ref.shape[0]

    def glu(a, g):
        return a * jax.nn.sigmoid(g)

    def put(row0, nrows, val):
        for gi in range(groups):
            ext_ref[gi, pl.ds(row0, nrows), :] = val[:, gi * LANES:(gi + 1) * LANES]

    put(CONV_HALO, tt, glu(a_ref[...], g_ref[...]))
    up = glu(ap_ref[...], gp_ref[...])
    put(0, CONV_HALO, jnp.where(ti > 0, up, jnp.zeros_like(up)))
    un = glu(an_ref[...], gn_ref[...])
    put(CONV_HALO + tt, CONV_HALO, jnp.where(ti < tiles_per_seq - 1, un, jnp.zeros_like(un)))

    first_tap = CONV_HALO - CONV_K // 2

    def chunk(c, carry):
        r0 = pl.multiple_of(c * CONV_ROWS, CONV_ROWS)
        accs = []
        for gi in range(groups):
            cols = slice(gi * LANES, (gi + 1) * LANES)
            acc = ext_ref[gi, pl.ds(r0 + first_tap, CONV_ROWS), :] * w_ref[0:1, cols]
            for k in range(1, CONV_K):
                acc = acc + ext_ref[gi, pl.ds(r0 + first_tap + k, CONV_ROWS), :] * w_ref[k:k + 1, cols]
            accs.append(acc)
        y = jnp.concatenate(accs, axis=-1) + b_ref[...]
        mu = jnp.mean(y, axis=-1, keepdims=True)
        yc = y - mu
        var = jnp.mean(yc * yc, axis=-1, keepdims=True)
        yn = yc * lax.rsqrt(var + LN_EPS) * lg_ref[...] + lb_ref[...]
        o_ref[pl.ds(r0, CONV_ROWS), :] = (yn * jax.nn.sigmoid(yn)).astype(o_ref.dtype)
        return carry

    lax.fori_loop(0, tt // CONV_ROWS, chunk, 0)


def _conv_group(ag, w_dw, b_dw, ln_g, ln_b, seq, tt=512):
    m = ag.shape[0]
    c = w_dw.shape[1]
    assert ag.shape[1] == 2 * c and seq % tt == 0 and tt % CONV_HALO == 0
    hb = tt // CONV_HALO
    last_hb = m // CONV_HALO - 1
    row = lambda v: v.reshape(1, c)
    main = lambda col: pl.BlockSpec((tt, c), lambda i: (i, col))
    prev = lambda col: pl.BlockSpec((CONV_HALO, c), lambda i: (jnp.maximum(i * hb - 1, 0), col))
    nxt = lambda col: pl.BlockSpec((CONV_HALO, c), lambda i: (jnp.minimum((i + 1) * hb, last_hb), col))
    vec = pl.BlockSpec((1, c), lambda i: (0, 0))
    est = 2 * (2 * _nbytes((tt, c), ag.dtype) + 4 * _nbytes((CONV_HALO, c), ag.dtype)
               + _nbytes((tt, c), jnp.bfloat16)) + 3 * _nbytes((tt + 2 * CONV_HALO, c), jnp.float32)
    return pl.pallas_call(
        functools.partial(_conv_kernel, tt=tt, tiles_per_seq=seq // tt),
        grid=(m // tt,),
        in_specs=[main(0), main(1), prev(0), prev(1), nxt(0), nxt(1),
                  pl.BlockSpec((CONV_K, c), lambda i: (0, 0)), vec, vec, vec],
        out_specs=pl.BlockSpec((tt, c), lambda i: (i, 0)),
        out_shape=jax.ShapeDtypeStruct((m, c), jnp.bfloat16),
        scratch_shapes=[pltpu.VMEM((c // LANES, tt + 2 * CONV_HALO, LANES), jnp.float32)],
        compiler_params=_params(("parallel",), est),
        name="conv_group",
    )(ag, ag, ag, ag, ag, ag, w_dw, row(b_dw), row(ln_g), row(ln_b))


def _bias_kernel(rpb_ref, o_ref, w_ref):
    p = pl.program_id(0)
    wq = lax.broadcasted_iota(jnp.int32, (GRID_W, PAIR_W), 0)
    lane = lax.broadcasted_iota(jnp.int32, (GRID_W, PAIR_W), 1)
    wk = jnp.bitwise_and(lane, GRID_W - 1)
    diff = wk - wq
    cs = jnp.clip(wq - WIN_COLS // 2, 0, GRID_W - WIN_COLS)
    valid = jnp.logical_and(wk >= cs, wk < cs + WIN_COLS)
    n_dr = 2 * WIN_ROWS - 1
    n_dc = 2 * WIN_COLS - 1
    for hh in range(2):
        base = (2 * p + hh) * (n_dr * n_dc)
        for dr in range(n_dr):
            acc = jnp.zeros((GRID_W, PAIR_W), jnp.float32)
            for v in range(n_dc):
                acc = jnp.where(diff == v - (WIN_COLS - 1), rpb_ref[base + dr * n_dc + v], acc)
            w_ref[dr] = jnp.where(valid, acc, NEG_INF)
        for oi in range(WIN_ROWS):
            for jj in range(WIN_ROWS // 2):
                dr0 = 2 * jj - oi + WIN_ROWS - 1
                tile = jnp.where(lane < GRID_W, w_ref[dr0], w_ref[dr0 + 1])
                o_ref[0, oi, hh * GRID_W:(hh + 1) * GRID_W, jj * PAIR_W:(jj + 1) * PAIR_W] = tile


def _bias_table(rpb_l):
    band = WIN_ROWS * GRID_W
    est = 2 * _nbytes((WIN_ROWS, PAIR_W, band), jnp.float32) + _nbytes((16, GRID_W, PAIR_W), jnp.float32)
    return pl.pallas_call(
        _bias_kernel,
        grid=(N_PAIRS,),
        in_specs=[pl.BlockSpec(memory_space=pltpu.SMEM)],
        out_specs=pl.BlockSpec((1, WIN_ROWS, PAIR_W, band), lambda p: (p, 0, 0, 0)),
        out_shape=jax.ShapeDtypeStruct((N_PAIRS, WIN_ROWS, PAIR_W, band), jnp.float32),
        scratch_shapes=[pltpu.VMEM((2 * WIN_ROWS - 1, GRID_W, PAIR_W), jnp.float32)],
        compiler_params=_params(("parallel",), est),
        name="bias_table",
    )(rpb_l.reshape(-1))


def _attn_kernel(q_ref, k_ref, v_ref, bias_ref, o_ref, *, rows):
    band = WIN_ROWS * GRID_W
    lo = lax.broadcasted_iota(jnp.int32, (GRID_W, PAIR_W), 1) < HEAD_DIM
    scale = HEAD_DIM ** -0.5

    def body(r, carry):
        rs = jnp.clip(r - WIN_ROWS // 2, 0, rows - WIN_ROWS)
        oi = r - rs
        q0 = pl.multiple_of(r * GRID_W, GRID_W)
        k0 = pl.multiple_of(rs * GRID_W, GRID_W)
        q2 = q_ref[pl.ds(q0, GRID_W), :]
        zero = jnp.zeros_like(q2)
        qs = jnp.concatenate([jnp.where(lo, q2, zero), jnp.where(lo, zero, q2)], axis=0)
        kb = k_ref[pl.ds(k0, band), :]
        vb = v_ref[pl.ds(k0, band), :]
        s = lax.dot_general(qs, kb, (((1,), (1,)), ((), ())),
                            preferred_element_type=jnp.float32) * scale
        s = s + bias_ref[0, oi]
        mx = jnp.max(s, axis=-1, keepdims=True)
        e = jnp.exp(s - mx)
        den = jnp.sum(e, axis=-1, keepdims=True)
        pv = jnp.dot(e.astype(vb.dtype), vb, preferred_element_type=jnp.float32) / den
        o = jnp.where(lo, pv[:GRID_W], pv[GRID_W:])
        o_ref[pl.ds(q0, GRID_W), :] = o.astype(o_ref.dtype)
        return carry

    lax.fori_loop(0, rows, body, 0)


def _attention(qkv, bias, batch, seq):
    m = qkv.shape[0]
    rows = seq // GRID_W
    assert rows >= WIN_ROWS and qkv.shape[1] == 3 * N_PAIRS * PAIR_W
    band = WIN_ROWS * GRID_W
    blk = lambda off: pl.BlockSpec((seq, PAIR_W), lambda b, p: (b, off + p))
    est = 2 * (4 * _nbytes((seq, PAIR_W), jnp.bfloat16) + _nbytes((WIN_ROWS, PAIR_W, band), jnp.float32))
    return pl.pallas_call(
        functools.partial(_attn_kernel, rows=rows),
        grid=(batch, N_PAIRS),
        in_specs=[blk(0), blk(N_PAIRS), blk(2 * N_PAIRS),
                  pl.BlockSpec((1, WIN_ROWS, PAIR_W, band), lambda b, p: (p, 0, 0, 0))],
        out_specs=pl.BlockSpec((seq, PAIR_W), lambda b, p: (b, p)),
        out_shape=jax.ShapeDtypeStruct((m, N_PAIRS * PAIR_W), jnp.bfloat16),
        compiler_params=_params(("parallel", "parallel"), est),
        name="natten",
    )(qkv, qkv, qkv, bias)


def _outproj_kernel(yc_ref, ya_ref, wt_ref, wb_ref, x_ref, g1_ref, g2_ref, xo_ref, h_ref):
    mix = jnp.dot(yc_ref[...], wt_ref[...], preferred_element_type=jnp.float32)
    mix = mix + jnp.dot(ya_ref[...], wb_ref[...], preferred_element_type=jnp.float32)
    xn = x_ref[...] + _rms(mix, g1_ref[...])
    xo_ref[...] = xn
    h_ref[...] = _rms(xn, g2_ref[...]).astype(h_ref.dtype)


def _outproj(yc, ya, w_top, w_bot, x2, g_post, g_next, tm=512):
    m, d = x2.shape
    c = yc.shape[1]
    est = 2 * (2 * _nbytes((tm, c), yc.dtype) + 2 * _nbytes((c, d), w_top.dtype)
               + 2 * _nbytes((tm, d), jnp.float32) + _nbytes((tm, d), jnp.bfloat16))
    est += 2 * _nbytes((tm, d), jnp.float32)
    act = lambda width: pl.BlockSpec((tm, width), lambda i: (i, 0))
    wspec = pl.BlockSpec((c, d), lambda i: (0, 0))
    vec = pl.BlockSpec((1, d), lambda i: (0, 0))
    return pl.pallas_call(
        _outproj_kernel,
        grid=(m // tm,),
        in_specs=[act(c), act(c), wspec, wspec, act(d), vec, vec],
        out_specs=[act(d), act(d)],
        out_shape=[jax.ShapeDtypeStruct((m, d), jnp.float32),
                   jax.ShapeDtypeStruct((m, d), jnp.bfloat16)],
        compiler_params=_params(("parallel",), est),
        name="outproj",
    )(yc, ya, w_top, w_bot, x2, g_post, g_next)


def _mlp_kernel(h_ref, wu_ref, wd_ref, x_ref, g1_ref, g2_ref, xo_ref, *rest, emit_next):
    if emit_next:
        hn_ref, acc_ref = rest
    else:
        (acc_ref,) = rest
    f = pl.program_id(1)
    hid = jnp.dot(h_ref[...], wu_ref[...], preferred_element_type=jnp.float32)
    hid = jnp.square(jnp.maximum(hid, 0.0)).astype(wd_ref.dtype)
    part = jnp.dot(hid, wd_ref[...], preferred_element_type=jnp.float32)

    @pl.when(f == 0)
    def _():
        acc_ref[...] = part

    @pl.when(f > 0)
    def _():
        acc_ref[...] += part

    @pl.when(f == pl.num_programs(1) - 1)
    def _():
        xn = x_ref[...] + _rms(acc_ref[...], g1_ref[...])
        xo_ref[...] = xn
        if emit_next:
            hn_ref[...] = _rms(xn, g2_ref[...]).astype(hn_ref.dtype)


def _mlp(h, w_up, w_down, x2, g_post, g_next, emit_next, tm=512, tf=512):
    m, d = x2.shape
    ff = w_up.shape[1]
    est = 2 * (_nbytes((tm, d), h.dtype) + _nbytes((d, tf), w_up.dtype) + _nbytes((tf, d), w_down.dtype)
               + 2 * _nbytes((tm, d), jnp.float32) + _nbytes((tm, d), jnp.bfloat16))
    est += _nbytes((tm, d), jnp.float32) + 2 * _nbytes((tm, tf), jnp.float32) + _nbytes((tm, d), jnp.float32)
    act = pl.BlockSpec((tm, d), lambda i, f: (i, 0))
    vec = pl.BlockSpec((1, d), lambda i, f: (0, 0))
    out_specs = [act]
    out_shape = [jax.ShapeDtypeStruct((m, d), jnp.float32)]
    if emit_next:
        out_specs.append(act)
        out_shape.append(jax.ShapeDtypeStruct((m, d), jnp.bfloat16))
    res = pl.pallas_call(
        functools.partial(_mlp_kernel, emit_next=emit_next),
        grid=(m // tm, ff // tf),
        in_specs=[act,
                  pl.BlockSpec((d, tf), lambda i, f: (0, f)),
                  pl.BlockSpec((tf, d), lambda i, f: (f, 0)),
                  act, vec, vec],
        out_specs=out_specs,
        out_shape=out_shape,
        scratch_shapes=[pltpu.VMEM((tm, d), jnp.float32)],
        compiler_params=_params(("parallel", "arbitrary"), est),
        name="mlp",
    )(h, w_up, w_down, x2, g_post, g_next)
    return (res[0], res[1]) if emit_next else (res[0], None)


def kernel(x, w_in, w_dw, b_dw, conv_ln_g, conv_ln_b, rpb, w_out, w_up, w_down,
           pre_mix_g, post_mix_g, pre_mlp_g, post_mlp_g):
    batch, seq, d = x.shape
    depth = w_in.shape[0]
    conv_w = w_dw.shape[2]
    assert seq % GRID_W == 0 and w_in.shape[2] == 2 * conv_w + 3 * NA_HEADS * HEAD_DIM
    bf16 = jnp.bfloat16
    row = lambda v: v.reshape(1, -1)

    x2 = x.reshape(batch * seq, d)
    h = _prenorm(x2, row(pre_mix_g[0]))
    for l in range(depth):
        w_in_l = w_in[l].astype(bf16)
        ag = _matmul(h, w_in_l[:, :2 * conv_w], jnp.float32, name="inproj_conv")
        qkv = _matmul(h, w_in_l[:, 2 * conv_w:], bf16, name="inproj_qkv")
        yc = _conv_group(ag, w_dw[l], b_dw[l], conv_ln_g[l], conv_ln_b[l], seq)
        ya = _attention(qkv, _bias_table(rpb[l]), batch, seq)
        w_out_l = w_out[l].astype(bf16)
        x2, h = _outproj(yc, ya, w_out_l[:conv_w], w_out_l[conv_w:], x2,
                         row(post_mix_g[l]), row(pre_mlp_g[l]))
        last = l == depth - 1
        g_next = row(post_mlp_g[l] if last else pre_mix_g[l + 1])
        x2, h = _mlp(h, w_up[l].astype(bf16), w_down[l].astype(bf16), x2,
                     row(post_mlp_g[l]), g_next, emit_next=not last)
    return x2.reshape(batch, seq, d)
```

```python
import functools

import jax
import jax.numpy as jnp
from jax import lax
from jax.experimental import pallas as pl
from jax.experimental.pallas import tpu as pltpu

LANES = 128
GRID_W = 64
NA_HEADS = 16
HEAD_DIM = 64
PAIR_W = 2 * HEAD_DIM
N_PAIRS = NA_HEADS // 2
CONV_K = 31
WIN_ROWS = 8
WIN_COLS = 16
RMS_EPS = 1e-6
LN_EPS = 1e-5
NEG_INF = -1e30

CONV_HALO = 16
CONV_ROWS = 32

V7X_VMEM_BYTES = 64 * 1024 * 1024
VMEM_RESERVE = 6 * 1024 * 1024


def _params(semantics, est_bytes):
    limit = min(int(est_bytes) + VMEM_RESERVE, V7X_VMEM_BYTES - VMEM_RESERVE)
    return pltpu.CompilerParams(dimension_semantics=semantics, vmem_limit_bytes=limit)


def _nbytes(shape, dtype):
    n = 1
    for s in shape:
        n *= s
    return n * jnp.dtype(dtype).itemsize


def _rms(xf, g):
    y = xf * lax.rsqrt(jnp.mean(xf * xf, axis=-1, keepdims=True) + RMS_EPS)
    return y * g


def _prenorm_kernel(x_ref, g_ref, h_ref):
    h_ref[...] = _rms(x_ref[...], g_ref[...]).astype(h_ref.dtype)


def _prenorm(x2, g, tm=512):
    m, d = x2.shape
    est = 2 * (_nbytes((tm, d), jnp.float32) + _nbytes((tm, d), jnp.bfloat16))
    return pl.pallas_call(
        _prenorm_kernel,
        grid=(m // tm,),
        in_specs=[pl.BlockSpec((tm, d), lambda i: (i, 0)),
                  pl.BlockSpec((1, d), lambda i: (0, 0))],
        out_specs=pl.BlockSpec((tm, d), lambda i: (i, 0)),
        out_shape=jax.ShapeDtypeStruct((m, d), jnp.bfloat16),
        compiler_params=_params(("parallel",), est),
        name="prenorm",
    )(x2, g)


def _matmul_kernel(h_ref, w_ref, o_ref):
    o_ref[...] = jnp.dot(h_ref[...], w_ref[...],
                         preferred_element_type=jnp.float32).astype(o_ref.dtype)


def _matmul(h, w, out_dtype, tm=1024, tn=1024, name="inproj"):
    m, k = h.shape
    n = w.shape[1]
    est = 2 * (_nbytes((tm, k), h.dtype) + _nbytes((k, tn), w.dtype) + _nbytes((tm, tn), out_dtype))
    est += _nbytes((tm, tn), jnp.float32)
    return pl.pallas_call(
        _matmul_kernel,
        grid=(m // tm, n // tn),
        in_specs=[pl.BlockSpec((tm, k), lambda i, j: (i, 0)),
                  pl.BlockSpec((k, tn), lambda i, j: (0, j))],
        out_specs=pl.BlockSpec((tm, tn), lambda i, j: (i, j)),
        out_shape=jax.ShapeDtypeStruct((m, n), out_dtype),
        compiler_params=_params(("parallel", "arbitrary"), est),
        name=name,
    )(h, w)


def _conv_kernel(a_ref, g_ref, ap_ref, gp_ref, an_ref, gn_ref, w_ref, b_ref, lg_ref, lb_ref,
                 o_ref, ext_ref, *, tt, tiles_per_seq):
    ti = lax.rem(pl.program_id(0), tiles_per_seq)
    groups = ext_ref.shape[0]

    def glu(a, g):
        return a * jax.nn.sigmoid(g)

    def put(row0, nrows, val):
        for gi in range(groups):
            ext_ref[gi, pl.ds(row0, nrows), :] = val[:, gi * LANES:(gi + 1) * LANES]

    put(CONV_HALO, tt, glu(a_ref[...], g_ref[...]))
    up = glu(ap_ref[...], gp_ref[...])
    put(0, CONV_HALO, jnp.where(ti > 0, up, jnp.zeros_like(up)))
    un = glu(an_ref[...], gn_ref[...])
    put(CONV_HALO + tt, CONV_HALO, jnp.where(ti < tiles_per_seq - 1, un, jnp.zeros_like(un)))

    first_tap = CONV_HALO - CONV_K // 2

    def chunk(c, carry):
        r0 = pl.multiple_of(c * CONV_ROWS, CONV_ROWS)
        accs = []
        for gi in range(groups):
            cols = slice(gi * LANES, (gi + 1) * LANES)
            acc = ext_ref[gi, pl.ds(r0 + first_tap, CONV_ROWS), :] * w_ref[0:1, cols]
            for k in range(1, CONV_K):
                acc = acc + ext_ref[gi, pl.ds(r0 + first_tap + k, CONV_ROWS), :] * w_ref[k:k + 1, cols]
            accs.append(acc)
        y = jnp.concatenate(accs, axis=-1) + b_ref[...]
        mu = jnp.mean(y, axis=-1, keepdims=True)
        yc = y - mu
        var = jnp.mean(yc * yc, axis=-1, keepdims=True)
        yn = yc * lax.rsqrt(var + LN_EPS) * lg_ref[...] + lb_ref[...]
        o_ref[pl.ds(r0, CONV_ROWS), :] = (yn * jax.nn.sigmoid(yn)).astype(o_ref.dtype)
        return carry

    lax.fori_loop(0, tt // CONV_ROWS, chunk, 0, unroll=2)


def _conv_group(ag, w_dw, b_dw, ln_g, ln_b, seq, tt=512):
    m = ag.shape[0]
    c = w_dw.shape[1]
    assert ag.shape[1] == 2 * c and seq % tt == 0 and tt % CONV_HALO == 0
    hb = tt // CONV_HALO
    last_hb = m // CONV_HALO - 1
    row = lambda v: v.reshape(1, c)
    main = lambda col: pl.BlockSpec((tt, c), lambda i: (i, col))
    prev = lambda col: pl.BlockSpec((CONV_HALO, c), lambda i: (jnp.maximum(i * hb - 1, 0), col))
    nxt = lambda col: pl.BlockSpec((CONV_HALO, c), lambda i: (jnp.minimum((i + 1) * hb, last_hb), col))
    vec = pl.BlockSpec((1, c), lambda i: (0, 0))
    est = 2 * (2 * _nbytes((tt, c), ag.dtype) + 4 * _nbytes((CONV_HALO, c), ag.dtype)
               + _nbytes((tt, c), jnp.bfloat16)) + 3 * _nbytes((tt + 2 * CONV_HALO, c), jnp.float32)
    return pl.pallas_call(
        functools.partial(_conv_kernel, tt=tt, tiles_per_seq=seq // tt),
        grid=(m // tt,),
        in_specs=[main(0), main(1), prev(0), prev(1), nxt(0), nxt(1),
                  pl.BlockSpec((CONV_K, c), lambda i: (0, 0)), vec, vec, vec],
        out_specs=pl.BlockSpec((tt, c), lambda i: (i, 0)),
        out_shape=jax.ShapeDtypeStruct((m, c), jnp.bfloat16),
        scratch_shapes=[pltpu.VMEM((c // LANES, tt + 2 * CONV_HALO, LANES), jnp.float32)],
        compiler_params=_params(("parallel",), est),
        name="conv_group",
    )(ag, ag, ag, ag, ag, ag, w_dw, row(b_dw), row(ln_g), row(ln_b))


def _bias_kernel(rpb_ref, o_ref, w_ref):
    p = pl.program_id(0)
    wq = lax.broadcasted_iota(jnp.int32, (GRID_W, PAIR_W), 0)
    lane = lax.broadcasted_iota(jnp.int32, (GRID_W, PAIR_W), 1)
    wk = jnp.bitwise_and(lane, GRID_W - 1)
    diff = wk - wq
    cs = jnp.clip(wq - WIN_COLS // 2, 0, GRID_W - WIN_COLS)
    valid = jnp.logical_and(wk >= cs, wk < cs + WIN_COLS)
    n_dr = 2 * WIN_ROWS - 1
    n_dc = 2 * WIN_COLS - 1
    for hh in range(2):
        base = (2 * p + hh) * (n_dr * n_dc)
        for dr in range(n_dr):
            acc = jnp.zeros((GRID_W, PAIR_W), jnp.float32)
            for v in range(n_dc):
                acc = jnp.where(diff == v - (WIN_COLS - 1), rpb_ref[base + dr * n_dc + v], acc)
            w_ref[dr] = jnp.where(valid, acc, NEG_INF)
        for oi in range(WIN_ROWS):
            for jj in range(WIN_ROWS // 2):
                dr0 = 2 * jj - oi + WIN_ROWS - 1
                tile = jnp.where(lane < GRID_W, w_ref[dr0], w_ref[dr0 + 1])
                o_ref[0, oi, hh * GRID_W:(hh + 1) * GRID_W, jj * PAIR_W:(jj + 1) * PAIR_W] = tile


def _bias_table(rpb_l):
    band = WIN_ROWS * GRID_W
    est = 2 * _nbytes((WIN_ROWS, PAIR_W, band), jnp.float32) + _nbytes((16, GRID_W, PAIR_W), jnp.float32)
    return pl.pallas_call(
        _bias_kernel,
        grid=(N_PAIRS,),
        in_specs=[pl.BlockSpec(memory_space=pltpu.SMEM)],
        out_specs=pl.BlockSpec((1, WIN_ROWS, PAIR_W, band), lambda p: (p, 0, 0, 0)),
        out_shape=jax.ShapeDtypeStruct((N_PAIRS, WIN_ROWS, PAIR_W, band), jnp.float32),
        scratch_shapes=[pltpu.VMEM((2 * WIN_ROWS - 1, GRID_W, PAIR_W), jnp.float32)],
        compiler_params=_params(("parallel",), est),
        name="bias_table",
    )(rpb_l.reshape(-1))


def _attn_kernel(q_ref, k_ref, v_ref, bias_ref, o_ref, *, rows):
    band = WIN_ROWS * GRID_W
    lo = lax.broadcasted_iota(jnp.int32, (GRID_W, PAIR_W), 1) < HEAD_DIM
    scale = HEAD_DIM ** -0.5

    def body(r, carry):
        rs = jnp.clip(r - WIN_ROWS // 2, 0, rows - WIN_ROWS)
        oi = r - rs
        q0 = pl.multiple_of(r * GRID_W, GRID_W)
        k0 = pl.multiple_of(rs * GRID_W, GRID_W)
        q2 = q_ref[pl.ds(q0, GRID_W), :]
        zero = jnp.zeros_like(q2)
        qs = jnp.concatenate([jnp.where(lo, q2, zero), jnp.where(lo, zero, q2)], axis=0)
        kb = k_ref[pl.ds(k0, band), :]
        vb = v_ref[pl.ds(k0, band), :]
        s = lax.dot_general(qs, kb, (((1,), (1,)), ((), ())),
                            preferred_element_type=jnp.float32) * scale
        s = s + bias_ref[0, oi]
        mx = jnp.max(s, axis=-1, keepdims=True)
        e = jnp.exp(s - mx)
        den = jnp.sum(e, axis=-1, keepdims=True)
        pv = jnp.dot(e.astype(vb.dtype), vb, preferred_element_type=jnp.float32) / den
        o = jnp.where(lo, pv[:GRID_W], pv[GRID_W:])
        o_ref[pl.ds(q0, GRID_W), :] = o.astype(o_ref.dtype)
        return carry

    lax.fori_loop(0, rows, body, 0, unroll=4)


def _attention(qkv, bias, batch, seq):
    m = qkv.shape[0]
    rows = seq // GRID_W
    assert rows >= WIN_ROWS and qkv.shape[1] == 3 * N_PAIRS * PAIR_W
    band = WIN_ROWS * GRID_W
    blk = lambda off: pl.BlockSpec((seq, PAIR_W), lambda b, p: (b, off + p))
    est = 2 * (4 * _nbytes((seq, PAIR_W), jnp.bfloat16) + _nbytes((WIN_ROWS, PAIR_W, band), jnp.float32))
    return pl.pallas_call(
        functools.partial(_attn_kernel, rows=rows),
        grid=(batch, N_PAIRS),
        in_specs=[blk(0), blk(N_PAIRS), blk(2 * N_PAIRS),
                  pl.BlockSpec((1, WIN_ROWS, PAIR_W, band), lambda b, p: (p, 0, 0, 0))],
        out_specs=pl.BlockSpec((seq, PAIR_W), lambda b, p: (b, p)),
        out_shape=jax.ShapeDtypeStruct((m, N_PAIRS * PAIR_W), jnp.bfloat16),
        compiler_params=_params(("parallel", "parallel"), est),
        name="natten",
    )(qkv, qkv, qkv, bias)


def _outproj_kernel(yc_ref, ya_ref, wt_ref, wb_ref, x_ref, g1_ref, g2_ref, xo_ref, h_ref):
    mix = jnp.dot(yc_ref[...], wt_ref[...], preferred_element_type=jnp.float32)
    mix = mix + jnp.dot(ya_ref[...], wb_ref[...], preferred_element_type=jnp.float32)
    xn = x_ref[...] + _rms(mix, g1_ref[...])
    xo_ref[...] = xn
    h_ref[...] = _rms(xn, g2_ref[...]).astype(h_ref.dtype)


def _outproj(yc, ya, w_top, w_bot, x2, g_post, g_next, tm=512):
    m, d = x2.shape
    c = yc.shape[1]
    est = 2 * (2 * _nbytes((tm, c), yc.dtype) + 2 * _nbytes((c, d), w_top.dtype)
               + 2 * _nbytes((tm, d), jnp.float32) + _nbytes((tm, d), jnp.bfloat16))
    est += 2 * _nbytes((tm, d), jnp.float32)
    act = lambda width: pl.BlockSpec((tm, width), lambda i: (i, 0))
    wspec = pl.BlockSpec((c, d), lambda i: (0, 0))
    vec = pl.BlockSpec((1, d), lambda i: (0, 0))
    return pl.pallas_call(
        _outproj_kernel,
        grid=(m // tm,),
        in_specs=[act(c), act(c), wspec, wspec, act(d), vec, vec],
        out_specs=[act(d), act(d)],
        out_shape=[jax.ShapeDtypeStruct((m, d), jnp.float32),
                   jax.ShapeDtypeStruct((m, d), jnp.bfloat16)],
        compiler_params=_params(("parallel",), est),
        name="outproj",
    )(yc, ya, w_top, w_bot, x2, g_post, g_next)


def _mlp_kernel(h_ref, wu_ref, wd_ref, x_ref, g1_ref, g2_ref, xo_ref, *rest, emit_next):
    if emit_next:
        hn_ref, acc_ref = rest
    else:
        (acc_ref,) = rest
    f = pl.program_id(1)

    @pl.when(f == 0)
    def _():
        acc_ref[...] = jnp.zeros_like(acc_ref)

    hid = jnp.dot(h_ref[...], wu_ref[...], preferred_element_type=jnp.float32)
    hid = jnp.square(jnp.maximum(hid, 0.0)).astype(wd_ref.dtype)
    acc_ref[...] += jnp.dot(hid, wd_ref[...], preferred_element_type=jnp.float32)

    @pl.when(f == pl.num_programs(1) - 1)
    def _():
        xn = x_ref[...] + _rms(acc_ref[...], g1_ref[...])
        xo_ref[...] = xn
        if emit_next:
            hn_ref[...] = _rms(xn, g2_ref[...]).astype(hn_ref.dtype)


def _mlp(h, w_up, w_down, x2, g_post, g_next, emit_next, tm=512, tf=1024):
    m, d = x2.shape
    ff = w_up.shape[1]
    est = 2 * (_nbytes((tm, d), h.dtype) + _nbytes((d, tf), w_up.dtype) + _nbytes((tf, d), w_down.dtype)
               + 2 * _nbytes((tm, d), jnp.float32) + _nbytes((tm, d), jnp.bfloat16))
    est += _nbytes((tm, d), jnp.float32) + 2 * _nbytes((tm, tf), jnp.float32) + _nbytes((tm, d), jnp.float32)
    act = pl.BlockSpec((tm, d), lambda i, f: (i, 0))
    vec = pl.BlockSpec((1, d), lambda i, f: (0, 0))
    out_specs = [act]
    out_shape = [jax.ShapeDtypeStruct((m, d), jnp.float32)]
    if emit_next:
        out_specs.append(act)
        out_shape.append(jax.ShapeDtypeStruct((m, d), jnp.bfloat16))
    res = pl.pallas_call(
        functools.partial(_mlp_kernel, emit_next=emit_next),
        grid=(m // tm, ff // tf),
        in_specs=[act,
                  pl.BlockSpec((d, tf), lambda i, f: (0, f)),
                  pl.BlockSpec((tf, d), lambda i, f: (f, 0)),
                  act, vec, vec],
        out_specs=out_specs,
        out_shape=out_shape,
        scratch_shapes=[pltpu.VMEM((tm, d), jnp.float32)],
        compiler_params=_params(("parallel", "arbitrary"), est),
        name="mlp",
    )(h, w_up, w_down, x2, g_post, g_next)
    return (res[0], res[1]) if emit_next else (res[0], None)


def kernel(x, w_in, w_dw, b_dw, conv_ln_g, conv_ln_b, rpb, w_out, w_up, w_down,
           pre_mix_g, post_mix_g, pre_mlp_g, post_mlp_g):
    batch, seq, d = x.shape
    depth = w_in.shape[0]
    conv_w = w_dw.shape[2]
    assert seq % GRID_W == 0 and w_in.shape[2] == 2 * conv_w + 3 * NA_HEADS * HEAD_DIM
    bf16 = jnp.bfloat16
    row = lambda v: v.reshape(1, -1)

    x2 = x.reshape(batch * seq, d)
    h = _prenorm(x2, row(pre_mix_g[0]))
    for l in range(depth):
        w_in_l = w_in[l].astype(bf16)
        ag = _matmul(h, w_in_l[:, :2 * conv_w], jnp.float32, name="inproj_conv")
        qkv = _matmul(h, w_in_l[:, 2 * conv_w:], bf16, name="inproj_qkv")
        yc = _conv_group(ag, w_dw[l], b_dw[l], conv_ln_g[l], conv_ln_b[l], seq)
        ya = _attention(qkv, _bias_table(rpb[l]), batch, seq)
        w_out_l = w_out[l].astype(bf16)
        x2, h = _outproj(yc, ya, w_out_l[:conv_w], w_out_l[conv_w:], x2,
                         row(post_mix_g[l]), row(pre_mlp_g[l]))
        last = l == depth - 1
        g_next = row(post_mlp_g[l] if last else pre_mix_g[l + 1])
        x2, h = _mlp(h, w_up[l].astype(bf16), w_down[l].astype(bf16), x2,
                     row(post_mlp_g[l]), g_next, emit_next=not last)
    return x2.reshape(batch, seq, d)
```

```python
import functools

import jax
import jax.numpy as jnp
from jax import lax
from jax.experimental import pallas as pl
from jax.experimental.pallas import tpu as pltpu

LANES = 128
SUBLANES = 8
GRID_W = 64
NA_HEADS = 16
HEAD_DIM = 64
PAIR_W = 2 * HEAD_DIM
N_PAIRS = NA_HEADS // 2
CONV_K = 31
WIN_ROWS = 8
WIN_COLS = 16
RMS_EPS = 1e-6
LN_EPS = 1e-5
NEG_INF = -1e30

CONV_HALO = 16
CONV_ROWS = 64
NORM_ROWS = 16
ATTN_ROWS = 4

V7X_VMEM_BYTES = 64 * 1024 * 1024
VMEM_RESERVE = 6 * 1024 * 1024


def _params(semantics, est_bytes):
    limit = min(int(est_bytes) + VMEM_RESERVE, V7X_VMEM_BYTES - VMEM_RESERVE)
    return pltpu.CompilerParams(dimension_semantics=semantics, vmem_limit_bytes=limit)


def _nbytes(shape, dtype):
    n = 1
    for s in shape:
        n *= s
    return n * jnp.dtype(dtype).itemsize


def _rms(xf, g):
    y = xf * lax.rsqrt(jnp.mean(xf * xf, axis=-1, keepdims=True) + RMS_EPS)
    return y * g


def _prenorm_kernel(x_ref, g_ref, h_ref):
    h_ref[...] = _rms(x_ref[...], g_ref[...]).astype(h_ref.dtype)


def _prenorm(x2, g, tm=512):
    m, d = x2.shape
    est = 2 * (_nbytes((tm, d), jnp.float32) + _nbytes((tm, d), jnp.bfloat16))
    return pl.pallas_call(
        _prenorm_kernel,
        grid=(m // tm,),
        in_specs=[pl.BlockSpec((tm, d), lambda i: (i, 0)),
                  pl.BlockSpec((1, d), lambda i: (0, 0))],
        out_specs=pl.BlockSpec((tm, d), lambda i: (i, 0)),
        out_shape=jax.ShapeDtypeStruct((m, d), jnp.bfloat16),
        compiler_params=_params(("parallel",), est),
        name="prenorm",
    )(x2, g)


def _matmul_kernel(h_ref, w_ref, o_ref):
    o_ref[...] = jnp.dot(h_ref[...], w_ref[...],
                         preferred_element_type=jnp.float32).astype(o_ref.dtype)


def _matmul(h, w_all, layer, col0, n, out_dtype, tm=2048, tn=1024, name="inproj"):
    m, k = h.shape
    assert col0 % tn == 0 and n % tn == 0 and m % tm == 0
    est = 2 * (_nbytes((tm, k), h.dtype) + _nbytes((k, tn), w_all.dtype) + _nbytes((tm, tn), out_dtype))
    est += _nbytes((tm, tn), jnp.float32)
    return pl.pallas_call(
        _matmul_kernel,
        grid=(m // tm, n // tn),
        in_specs=[pl.BlockSpec((tm, k), lambda i, j: (i, 0)),
                  pl.BlockSpec((None, k, tn), lambda i, j: (layer, 0, col0 // tn + j))],
        out_specs=pl.BlockSpec((tm, tn), lambda i, j: (i, j)),
        out_shape=jax.ShapeDtypeStruct((m, n), out_dtype),
        compiler_params=_params(("parallel", "arbitrary"), est),
        name=name,
    )(h, w_all)


def _conv_kernel(a_ref, g_ref, ap_ref, gp_ref, an_ref, gn_ref, w_ref, b_ref, lg_ref, lb_ref,
                 o_ref, ext_ref, y_ref, *, tt, tiles_per_seq):
    ti = lax.rem(pl.program_id(0), tiles_per_seq)
    groups = ext_ref.shape[0]

    def glu(a, g):
        return a * jax.nn.sigmoid(g)

    def put(row0, nrows, val):
        for gi in range(groups):
            ext_ref[gi, pl.ds(row0, nrows), :] = val[:, gi * LANES:(gi + 1) * LANES]

    put(CONV_HALO, tt, glu(a_ref[...], g_ref[...]))
    up = glu(ap_ref[...], gp_ref[...])
    put(0, CONV_HALO, jnp.where(ti > 0, up, jnp.zeros_like(up)))
    un = glu(an_ref[...], gn_ref[...])
    put(CONV_HALO + tt, CONV_HALO, jnp.where(ti < tiles_per_seq - 1, un, jnp.zeros_like(un)))

    first_tap = CONV_HALO - CONV_K // 2
    sub_tiles = CONV_ROWS // SUBLANES

    for gi in range(groups):
        cols = slice(gi * LANES, (gi + 1) * LANES)
        taps = [jnp.broadcast_to(w_ref[k:k + 1, cols], (SUBLANES, LANES)) for k in range(CONV_K)]

        def conv_chunk(c, carry, gi=gi, cols=cols, taps=taps):
            r0 = pl.multiple_of(c * CONV_ROWS, CONV_ROWS)
            for j in range(sub_tiles):
                row = r0 + first_tap + j * SUBLANES
                acc = ext_ref[gi, pl.ds(row, SUBLANES), :] * taps[0]
                for k in range(1, CONV_K):
                    acc = acc + ext_ref[gi, pl.ds(row + k, SUBLANES), :] * taps[k]
                y_ref[pl.ds(r0 + j * SUBLANES, SUBLANES), cols] = acc
            return carry

        lax.fori_loop(0, tt // CONV_ROWS, conv_chunk, 0)

    def norm_chunk(c, carry):
        r0 = pl.multiple_of(c * NORM_ROWS, NORM_ROWS)
        y = y_ref[pl.ds(r0, NORM_ROWS), :] + b_ref[...]
        mu = jnp.mean(y, axis=-1, keepdims=True)
        yc = y - mu
        var = jnp.mean(yc * yc, axis=-1, keepdims=True)
        yn = yc * lax.rsqrt(var + LN_EPS) * lg_ref[...] + lb_ref[...]
        o_ref[pl.ds(r0, NORM_ROWS), :] = (yn * jax.nn.sigmoid(yn)).astype(o_ref.dtype)
        return carry

    lax.fori_loop(0, tt // NORM_ROWS, norm_chunk, 0, unroll=8)


def _conv_group(ag, w_dw, b_dw, ln_g, ln_b, layer, seq, tt=512):
    m = ag.shape[0]
    c = w_dw.shape[2]
    assert ag.shape[1] == 2 * c and seq % tt == 0 and tt % CONV_HALO == 0
    hb = tt // CONV_HALO
    last_hb = m // CONV_HALO - 1
    main = lambda col: pl.BlockSpec((tt, c), lambda i: (i, col))
    prev = lambda col: pl.BlockSpec((CONV_HALO, c), lambda i: (jnp.maximum(i * hb - 1, 0), col))
    nxt = lambda col: pl.BlockSpec((CONV_HALO, c), lambda i: (jnp.minimum((i + 1) * hb, last_hb), col))
    vec = pl.BlockSpec((None, 1, c), lambda i: (layer, 0, 0))
    est = 2 * (2 * _nbytes((tt, c), ag.dtype) + 4 * _nbytes((CONV_HALO, c), ag.dtype)
               + _nbytes((tt, c), jnp.bfloat16)) + 3 * _nbytes((tt + 2 * CONV_HALO, c), jnp.float32)
    return pl.pallas_call(
        functools.partial(_conv_kernel, tt=tt, tiles_per_seq=seq // tt),
        grid=(m // tt,),
        in_specs=[main(0), main(1), prev(0), prev(1), nxt(0), nxt(1),
                  pl.BlockSpec((None, CONV_K, c), lambda i: (layer, 0, 0)), vec, vec, vec],
        out_specs=pl.BlockSpec((tt, c), lambda i: (i, 0)),
        out_shape=jax.ShapeDtypeStruct((m, c), jnp.bfloat16),
        scratch_shapes=[pltpu.VMEM((c // LANES, tt + 2 * CONV_HALO, LANES), jnp.float32),
                        pltpu.VMEM((tt, c), jnp.float32)],
        compiler_params=_params(("parallel",), est),
        name="conv_group",
    )(ag, ag, ag, ag, ag, ag, w_dw, b_dw, ln_g, ln_b)


def _bias_kernel(rpb_ref, o_ref, w_ref):
    p = pl.program_id(0)
    wq = lax.broadcasted_iota(jnp.int32, (GRID_W, PAIR_W), 0)
    lane = lax.broadcasted_iota(jnp.int32, (GRID_W, PAIR_W), 1)
    wk = jnp.bitwise_and(lane, GRID_W - 1)
    diff = wk - wq
    cs = jnp.clip(wq - WIN_COLS // 2, 0, GRID_W - WIN_COLS)
    valid = jnp.logical_and(wk >= cs, wk < cs + WIN_COLS)
    n_dr = 2 * WIN_ROWS - 1
    n_dc = 2 * WIN_COLS - 1
    for hh in range(2):
        base = (2 * p + hh) * (n_dr * n_dc)
        for dr in range(n_dr):
            acc = jnp.zeros((GRID_W, PAIR_W), jnp.float32)
            for v in range(n_dc):
                acc = jnp.where(diff == v - (WIN_COLS - 1), rpb_ref[base + dr * n_dc + v], acc)
            w_ref[dr] = jnp.where(valid, acc, NEG_INF)
        for oi in range(WIN_ROWS):
            for jj in range(WIN_ROWS // 2):
                dr0 = 2 * jj - oi + WIN_ROWS - 1
                tile = jnp.where(lane < GRID_W, w_ref[dr0], w_ref[dr0 + 1])
                o_ref[0, oi, hh * GRID_W:(hh + 1) * GRID_W, jj * PAIR_W:(jj + 1) * PAIR_W] = tile


def _bias_table(rpb_l):
    band = WIN_ROWS * GRID_W
    est = 2 * _nbytes((WIN_ROWS, PAIR_W, band), jnp.float32) + _nbytes((16, GRID_W, PAIR_W), jnp.float32)
    return pl.pallas_call(
        _bias_kernel,
        grid=(N_PAIRS,),
        in_specs=[pl.BlockSpec(memory_space=pltpu.SMEM)],
        out_specs=pl.BlockSpec((1, WIN_ROWS, PAIR_W, band), lambda p: (p, 0, 0, 0)),
        out_shape=jax.ShapeDtypeStruct((N_PAIRS, WIN_ROWS, PAIR_W, band), jnp.float32),
        scratch_shapes=[pltpu.VMEM((2 * WIN_ROWS - 1, GRID_W, PAIR_W), jnp.float32)],
        compiler_params=_params(("parallel",), est),
        name="bias_table",
    )(rpb_l.reshape(-1))


def _attn_kernel(q_ref, k_ref, v_ref, bias_ref, o_ref, s_ref, *, rows):
    band = WIN_ROWS * GRID_W
    lo = lax.broadcasted_iota(jnp.int32, (GRID_W, PAIR_W), 1) < HEAD_DIM
    scale = HEAD_DIM ** -0.5
    n_groups = rows // ATTN_ROWS

    def offsets(g, i):
        r = g * ATTN_ROWS + i
        rs = jnp.clip(r - WIN_ROWS // 2, 0, rows - WIN_ROWS)
        return (r - rs, pl.multiple_of(r * GRID_W, GRID_W), pl.multiple_of(rs * GRID_W, GRID_W))

    def scores(g, slot):
        for i in range(ATTN_ROWS):
            oi, q0, k0 = offsets(g, i)
            q2 = q_ref[pl.ds(q0, GRID_W), :]
            zero = jnp.zeros_like(q2)
            qs = jnp.concatenate([jnp.where(lo, q2, zero), jnp.where(lo, zero, q2)], axis=0)
            s = lax.dot_general(qs, k_ref[pl.ds(k0, band), :], (((1,), (1,)), ((), ())),
                                preferred_element_type=jnp.float32)
            s_ref[slot, i] = s * scale + bias_ref[0, oi]

    def finish(g, slot):
        for i in range(ATTN_ROWS):
            _, q0, k0 = offsets(g, i)
            s = s_ref[slot, i]
            e = jnp.exp(s - jnp.max(s, axis=-1, keepdims=True))
            den = jnp.sum(e, axis=-1, keepdims=True)
            vb = v_ref[pl.ds(k0, band), :]
            pv = jnp.dot(e.astype(vb.dtype), vb, preferred_element_type=jnp.float32) / den
            o = jnp.where(lo, pv[:GRID_W], pv[GRID_W:])
            o_ref[pl.ds(q0, GRID_W), :] = o.astype(o_ref.dtype)

    scores(0, 0)

    def body(gp, carry):
        g = 2 * gp
        scores(g + 1, 1)
        finish(g, 0)
        scores(jnp.minimum(g + 2, n_groups - 1), 0)
        finish(g + 1, 1)
        return carry

    lax.fori_loop(0, n_groups // 2, body, 0)


def _attention(qkv, bias, batch, seq):
    m = qkv.shape[0]
    rows = seq // GRID_W
    assert rows >= WIN_ROWS and rows % (2 * ATTN_ROWS) == 0 and qkv.shape[1] == 3 * N_PAIRS * PAIR_W
    band = WIN_ROWS * GRID_W
    blk = lambda off: pl.BlockSpec((seq, PAIR_W), lambda b, p: (b, off + p))
    s_shape = (2, ATTN_ROWS, PAIR_W, band)
    est = 2 * (4 * _nbytes((seq, PAIR_W), jnp.bfloat16) + _nbytes((WIN_ROWS, PAIR_W, band), jnp.float32))
    est += 2 * _nbytes(s_shape, jnp.float32)
    return pl.pallas_call(
        functools.partial(_attn_kernel, rows=rows),
        grid=(batch, N_PAIRS),
        in_specs=[blk(0), blk(N_PAIRS), blk(2 * N_PAIRS),
                  pl.BlockSpec((1, WIN_ROWS, PAIR_W, band), lambda b, p: (p, 0, 0, 0))],
        out_specs=pl.BlockSpec((seq, PAIR_W), lambda b, p: (b, p)),
        out_shape=jax.ShapeDtypeStruct((m, N_PAIRS * PAIR_W), jnp.bfloat16),
        scratch_shapes=[pltpu.VMEM(s_shape, jnp.float32)],
        compiler_params=_params(("parallel", "parallel"), est),
        name="natten",
    )(qkv, qkv, qkv, bias)


def _outproj_kernel(yc_ref, ya_ref, wt_ref, wb_ref, x_ref, g1_ref, g2_ref, xo_ref, h_ref):
    mix = jnp.dot(yc_ref[...], wt_ref[...], preferred_element_type=jnp.float32)
    mix = mix + jnp.dot(ya_ref[...], wb_ref[...], preferred_element_type=jnp.float32)
    xn = x_ref[...] + _rms(mix, g1_ref[...])
    xo_ref[...] = xn
    h_ref[...] = _rms(xn, g2_ref[...]).astype(h_ref.dtype)


def _outproj(yc, ya, w_all, layer, x2, g_post, g_next, tm=512):
    m, d = x2.shape
    c = yc.shape[1]
    est = 2 * (2 * _nbytes((tm, c), yc.dtype) + 2 * _nbytes((c, d), w_all.dtype)
               + 2 * _nbytes((tm, d), jnp.float32) + _nbytes((tm, d), jnp.bfloat16))
    est += 2 * _nbytes((tm, d), jnp.float32)
    act = lambda width: pl.BlockSpec((tm, width), lambda i: (i, 0))
    wspec = lambda half: pl.BlockSpec((None, c, d), lambda i: (layer, half, 0))
    vec = pl.BlockSpec((1, d), lambda i: (0, 0))
    return pl.pallas_call(
        _outproj_kernel,
        grid=(m // tm,),
        in_specs=[act(c), act(c), wspec(0), wspec(1), act(d), vec, vec],
        out_specs=[act(d), act(d)],
        out_shape=[jax.ShapeDtypeStruct((m, d), jnp.float32),
                   jax.ShapeDtypeStruct((m, d), jnp.bfloat16)],
        compiler_params=_params(("parallel",), est),
        name="outproj",
    )(yc, ya, w_all, w_all, x2, g_post, g_next)


def _mlp_kernel(h_ref, wu_ref, wd_ref, x_ref, g1_ref, g2_ref, xo_ref, *rest, emit_next):
    f = pl.program_id(1)

    @pl.when(f == 0)
    def _():
        xo_ref[...] = jnp.zeros_like(xo_ref)

    hid = jnp.dot(h_ref[...], wu_ref[...], preferred_element_type=jnp.float32)
    hid = jnp.square(jnp.maximum(hid, 0.0)).astype(wd_ref.dtype)
    xo_ref[...] += jnp.dot(hid, wd_ref[...], preferred_element_type=jnp.float32)

    @pl.when(f == pl.num_programs(1) - 1)
    def _():
        xn = x_ref[...] + _rms(xo_ref[...], g1_ref[...])
        xo_ref[...] = xn
        if emit_next:
            (hn_ref,) = rest
            hn_ref[...] = _rms(xn, g2_ref[...]).astype(hn_ref.dtype)


def _mlp(h, w_up_all, w_down_all, layer, x2, g_post, g_next, emit_next, tm=1024, tf=512):
    m, d = x2.shape
    est = (2 * _nbytes((tm, d), h.dtype) + _nbytes((tm, d), jnp.float32)
           + 2 * (_nbytes((d, tf), w_up_all.dtype) + _nbytes((tf, d), w_down_all.dtype))
           + 2 * _nbytes((tm, d), jnp.float32) + 2 * _nbytes((tm, d), jnp.bfloat16) * emit_next)
    est += 2 * _nbytes((tm, tf), jnp.float32)
    act = pl.BlockSpec((tm, d), lambda i, f: (i, 0))
    vec = pl.BlockSpec((1, d), lambda i, f: (0, 0))
    out_specs = [act]
    out_shape = [jax.ShapeDtypeStruct((m, d), jnp.float32)]
    if emit_next:
        out_specs.append(act)
        out_shape.append(jax.ShapeDtypeStruct((m, d), jnp.bfloat16))
    res = pl.pallas_call(
        functools.partial(_mlp_kernel, emit_next=emit_next),
        grid=(m // tm, w_up_all.shape[2] // tf),
        in_specs=[act,
                  pl.BlockSpec((None, d, tf), lambda i, f: (layer, 0, f)),
                  pl.BlockSpec((None, tf, d), lambda i, f: (layer, f, 0)),
                  pl.BlockSpec((tm, d), lambda i, f: (i, 0), pipeline_mode=pl.Buffered(1)),
                  vec, vec],
        out_specs=out_specs,
        out_shape=out_shape,
        compiler_params=_params(("parallel", "arbitrary"), est),
        name="mlp",
    )(h, w_up_all, w_down_all, x2, g_post, g_next)
    return (res[0], res[1]) if emit_next else (res[0], None)


def kernel(x, w_in, w_dw, b_dw, conv_ln_g, conv_ln_b, rpb, w_out, w_up, w_down,
           pre_mix_g, post_mix_g, pre_mlp_g, post_mlp_g):
    batch, seq, d = x.shape
    depth = w_in.shape[0]
    conv_w = w_dw.shape[2]
    assert seq % GRID_W == 0 and w_in.shape[2] == 2 * conv_w + 3 * NA_HEADS * HEAD_DIM
    bf16 = jnp.bfloat16
    row = lambda v: v.reshape(1, -1)
    rows3 = lambda v: v.reshape(depth, 1, -1)

    w_in_b, w_out_b, w_up_b, w_down_b = (w.astype(bf16) for w in (w_in, w_out, w_up, w_down))
    b_dw3, ln_g3, ln_b3 = rows3(b_dw), rows3(conv_ln_g), rows3(conv_ln_b)
    qkv_w = 3 * NA_HEADS * HEAD_DIM

    x2 = x.reshape(batch * seq, d)
    h = _prenorm(x2, row(pre_mix_g[0]))
    for l in range(depth):
        ag = _matmul(h, w_in_b, l, 0, 2 * conv_w, jnp.float32, name="inproj_conv")
        qkv = _matmul(h, w_in_b, l, 2 * conv_w, qkv_w, bf16, name="inproj_qkv")
        yc = _conv_group(ag, w_dw, b_dw3, ln_g3, ln_b3, l, seq)
        ya = _attention(qkv, _bias_table(rpb[l]), batch, seq)
        x2, h = _outproj(yc, ya, w_out_b, l, x2, row(post_mix_g[l]), row(pre_mlp_g[l]))
        last = l == depth - 1
        g_next = row(post_mlp_g[l] if last else pre_mix_g[l + 1])
        x2, h = _mlp(h, w_up_b, w_down_b, l, x2, row(post_mlp_g[l]), g_next, emit_next=not last)
    return x2.reshape(batch, seq, d)
```

```python
import functools

import jax
import jax.numpy as jnp
from jax import lax
from jax.experimental import pallas as pl
from jax.experimental.pallas import tpu as pltpu

LANES = 128
SUBLANES = 8
GRID_W = 64
NA_HEADS = 16
HEAD_DIM = 64
PAIR_W = 2 * HEAD_DIM
N_PAIRS = NA_HEADS // 2
CONV_K = 31
WIN_ROWS = 8
WIN_COLS = 16
RMS_EPS = 1e-6
LN_EPS = 1e-5
NEG_INF = -1e30
LOG2_E = 1.4426950408889634

CONV_HALO = 16
CONV_ROWS = 64
NORM_ROWS = 16
OUTPROJ_SUBBLOCKS = 2
ATTN_ROWS = 2

V7X_VMEM_BYTES = 64 * 1024 * 1024
VMEM_RESERVE = 6 * 1024 * 1024


def _params(semantics, est_bytes):
    limit = min(int(est_bytes) + VMEM_RESERVE, V7X_VMEM_BYTES - VMEM_RESERVE)
    return pltpu.CompilerParams(dimension_semantics=semantics, vmem_limit_bytes=limit)


def _nbytes(shape, dtype):
    n = 1
    for s in shape:
        n *= s
    return n * jnp.dtype(dtype).itemsize


def _sigmoid(x):
    return 0.5 * jnp.tanh(0.5 * x) + 0.5


def _rms(xf, g):
    y = xf * lax.rsqrt(jnp.mean(xf * xf, axis=-1, keepdims=True) + RMS_EPS)
    return y * g


def _prenorm_kernel(x_ref, g_ref, h_ref):
    h_ref[...] = _rms(x_ref[...], g_ref[...]).astype(h_ref.dtype)


def _prenorm(x2, g, tm=512):
    m, d = x2.shape
    est = 2 * (_nbytes((tm, d), jnp.float32) + _nbytes((tm, d), jnp.bfloat16))
    return pl.pallas_call(
        _prenorm_kernel,
        grid=(m // tm,),
        in_specs=[pl.BlockSpec((tm, d), lambda i: (i, 0)),
                  pl.BlockSpec((1, d), lambda i: (0, 0))],
        out_specs=pl.BlockSpec((tm, d), lambda i: (i, 0)),
        out_shape=jax.ShapeDtypeStruct((m, d), jnp.bfloat16),
        compiler_params=_params(("parallel",), est),
        name="prenorm",
    )(x2, g)


def _matmul_kernel(h_ref, w_ref, o_ref):
    o_ref[...] = jnp.dot(h_ref[...], w_ref[...],
                         preferred_element_type=jnp.float32).astype(o_ref.dtype)


def _matmul(h, w_all, layer, col0, n, out_dtype, tm=2048, tn=1024, name="inproj"):
    m, k = h.shape
    assert col0 % tn == 0 and n % tn == 0 and m % tm == 0
    est = 2 * (_nbytes((tm, k), h.dtype) + _nbytes((k, tn), w_all.dtype) + _nbytes((tm, tn), out_dtype))
    est += _nbytes((tm, tn), jnp.float32)
    return pl.pallas_call(
        _matmul_kernel,
        grid=(m // tm, n // tn),
        in_specs=[pl.BlockSpec((tm, k), lambda i, j: (i, 0)),
                  pl.BlockSpec((None, k, tn), lambda i, j: (layer, 0, col0 // tn + j))],
        out_specs=pl.BlockSpec((tm, tn), lambda i, j: (i, j)),
        out_shape=jax.ShapeDtypeStruct((m, n), out_dtype),
        compiler_params=_params(("parallel", "arbitrary"), est),
        name=name,
    )(h, w_all)


def _conv_kernel(a_ref, g_ref, ap_ref, gp_ref, an_ref, gn_ref, w_ref, b_ref, lg_ref, lb_ref,
                 o_ref, ext_ref, y_ref, *, tt, tiles_per_seq):
    ti = lax.rem(pl.program_id(0), tiles_per_seq)
    groups = ext_ref.shape[0]

    def glu(a, g):
        return a * _sigmoid(g)

    def put(row0, nrows, val):
        for gi in range(groups):
            ext_ref[gi, pl.ds(row0, nrows), :] = val[:, gi * LANES:(gi + 1) * LANES]

    put(CONV_HALO, tt, glu(a_ref[...], g_ref[...]))
    up = glu(ap_ref[...], gp_ref[...])
    put(0, CONV_HALO, jnp.where(ti > 0, up, jnp.zeros_like(up)))
    un = glu(an_ref[...], gn_ref[...])
    put(CONV_HALO + tt, CONV_HALO, jnp.where(ti < tiles_per_seq - 1, un, jnp.zeros_like(un)))

    first_tap = CONV_HALO - CONV_K // 2
    sub_tiles = CONV_ROWS // SUBLANES

    for gi in range(groups):
        cols = slice(gi * LANES, (gi + 1) * LANES)
        taps = [jnp.broadcast_to(w_ref[k:k + 1, cols], (SUBLANES, LANES)) for k in range(CONV_K)]

        def conv_chunk(c, carry, gi=gi, cols=cols, taps=taps):
            r0 = pl.multiple_of(c * CONV_ROWS, CONV_ROWS)
            for j in range(sub_tiles):
                row = r0 + first_tap + j * SUBLANES
                acc = ext_ref[gi, pl.ds(row, SUBLANES), :] * taps[0]
                for k in range(1, CONV_K):
                    acc = acc + ext_ref[gi, pl.ds(row + k, SUBLANES), :] * taps[k]
                y_ref[pl.ds(r0 + j * SUBLANES, SUBLANES), cols] = acc
            return carry

        lax.fori_loop(0, tt // CONV_ROWS, conv_chunk, 0)

    def norm_chunk(c, carry):
        r0 = pl.multiple_of(c * NORM_ROWS, NORM_ROWS)
        y = y_ref[pl.ds(r0, NORM_ROWS), :] + b_ref[...]
        mu = jnp.mean(y, axis=-1, keepdims=True)
        yc = y - mu
        var = jnp.mean(yc * yc, axis=-1, keepdims=True)
        yn = yc * lax.rsqrt(var + LN_EPS) * lg_ref[...] + lb_ref[...]
        o_ref[pl.ds(r0, NORM_ROWS), :] = (yn * _sigmoid(yn)).astype(o_ref.dtype)
        return carry

    lax.fori_loop(0, tt // NORM_ROWS, norm_chunk, 0, unroll=8)


def _conv_group(ag, w_dw, b_dw, ln_g, ln_b, layer, seq, tt=512):
    m = ag.shape[0]
    c = w_dw.shape[2]
    assert ag.shape[1] == 2 * c and seq % tt == 0 and tt % CONV_HALO == 0
    hb = tt // CONV_HALO
    last_hb = m // CONV_HALO - 1
    main = lambda col: pl.BlockSpec((tt, c), lambda i: (i, col))
    prev = lambda col: pl.BlockSpec((CONV_HALO, c), lambda i: (jnp.maximum(i * hb - 1, 0), col))
    nxt = lambda col: pl.BlockSpec((CONV_HALO, c), lambda i: (jnp.minimum((i + 1) * hb, last_hb), col))
    vec = pl.BlockSpec((None, 1, c), lambda i: (layer, 0, 0))
    est = 2 * (2 * _nbytes((tt, c), ag.dtype) + 4 * _nbytes((CONV_HALO, c), ag.dtype)
               + _nbytes((tt, c), jnp.bfloat16)) + 3 * _nbytes((tt + 2 * CONV_HALO, c), jnp.float32)
    return pl.pallas_call(
        functools.partial(_conv_kernel, tt=tt, tiles_per_seq=seq // tt),
        grid=(m // tt,),
        in_specs=[main(0), main(1), prev(0), prev(1), nxt(0), nxt(1),
                  pl.BlockSpec((None, CONV_K, c), lambda i: (layer, 0, 0)), vec, vec, vec],
        out_specs=pl.BlockSpec((tt, c), lambda i: (i, 0)),
        out_shape=jax.ShapeDtypeStruct((m, c), jnp.bfloat16),
        scratch_shapes=[pltpu.VMEM((c // LANES, tt + 2 * CONV_HALO, LANES), jnp.float32),
                        pltpu.VMEM((tt, c), jnp.float32)],
        compiler_params=_params(("parallel",), est),
        name="conv_group",
    )(ag, ag, ag, ag, ag, ag, w_dw, b_dw, ln_g, ln_b)


def _bias_kernel(rpb_ref, o_ref, w_ref):
    p = pl.program_id(0)
    wq = lax.broadcasted_iota(jnp.int32, (GRID_W, PAIR_W), 0)
    lane = lax.broadcasted_iota(jnp.int32, (GRID_W, PAIR_W), 1)
    wk = jnp.bitwise_and(lane, GRID_W - 1)
    diff = wk - wq
    cs = jnp.clip(wq - WIN_COLS // 2, 0, GRID_W - WIN_COLS)
    valid = jnp.logical_and(wk >= cs, wk < cs + WIN_COLS)
    n_dr = 2 * WIN_ROWS - 1
    n_dc = 2 * WIN_COLS - 1
    for hh in range(2):
        base = (2 * p + hh) * (n_dr * n_dc)
        for dr in range(n_dr):
            acc = jnp.zeros((GRID_W, PAIR_W), jnp.float32)
            for v in range(n_dc):
                acc = jnp.where(diff == v - (WIN_COLS - 1), rpb_ref[base + dr * n_dc + v], acc)
            w_ref[dr] = jnp.where(valid, acc * LOG2_E, NEG_INF)
        for oi in range(WIN_ROWS):
            for jj in range(WIN_ROWS // 2):
                dr0 = 2 * jj - oi + WIN_ROWS - 1
                tile = jnp.where(lane < GRID_W, w_ref[dr0], w_ref[dr0 + 1])
                o_ref[0, oi, hh * GRID_W:(hh + 1) * GRID_W, jj * PAIR_W:(jj + 1) * PAIR_W] = tile


def _bias_table(rpb_l):
    band = WIN_ROWS * GRID_W
    est = 2 * _nbytes((WIN_ROWS, PAIR_W, band), jnp.float32) + _nbytes((16, GRID_W, PAIR_W), jnp.float32)
    return pl.pallas_call(
        _bias_kernel,
        grid=(N_PAIRS,),
        in_specs=[pl.BlockSpec(memory_space=pltpu.SMEM)],
        out_specs=pl.BlockSpec((1, WIN_ROWS, PAIR_W, band), lambda p: (p, 0, 0, 0)),
        out_shape=jax.ShapeDtypeStruct((N_PAIRS, WIN_ROWS, PAIR_W, band), jnp.float32),
        scratch_shapes=[pltpu.VMEM((2 * WIN_ROWS - 1, GRID_W, PAIR_W), jnp.float32)],
        compiler_params=_params(("parallel",), est),
        name="bias_table",
    )(rpb_l.reshape(-1))


def _attn_kernel(q_ref, k_ref, v_ref, bias_ref, o_ref, s_ref, *, rows):
    band = WIN_ROWS * GRID_W
    lo = lax.broadcasted_iota(jnp.int32, (GRID_W, PAIR_W), 1) < HEAD_DIM
    scale = HEAD_DIM ** -0.5 * LOG2_E
    n_groups = rows // ATTN_ROWS

    def offsets(g, i):
        r = g * ATTN_ROWS + i
        rs = jnp.clip(r - WIN_ROWS // 2, 0, rows - WIN_ROWS)
        return (r - rs, pl.multiple_of(r * GRID_W, GRID_W), pl.multiple_of(rs * GRID_W, GRID_W))

    def scores(g, slot):
        for i in range(ATTN_ROWS):
            oi, q0, k0 = offsets(g, i)
            q2 = q_ref[pl.ds(q0, GRID_W), :]
            zero = jnp.zeros_like(q2)
            qs = jnp.concatenate([jnp.where(lo, q2, zero), jnp.where(lo, zero, q2)], axis=0)
            s = lax.dot_general(qs, k_ref[pl.ds(k0, band), :], (((1,), (1,)), ((), ())),
                                preferred_element_type=jnp.float32)
            s_ref[slot, i] = s * scale + bias_ref[0, oi]

    def finish(g, slot):
        for i in range(ATTN_ROWS):
            _, q0, k0 = offsets(g, i)
            s = s_ref[slot, i]
            e = jnp.exp2(s - jnp.max(s, axis=-1, keepdims=True))
            den = jnp.sum(e, axis=-1, keepdims=True)
            vb = v_ref[pl.ds(k0, band), :]
            pv = jnp.dot(e.astype(vb.dtype), vb, preferred_element_type=jnp.float32) / den
            o = jnp.where(lo, pv[:GRID_W], pv[GRID_W:])
            o_ref[pl.ds(q0, GRID_W), :] = o.astype(o_ref.dtype)

    scores(0, 0)

    def body(gp, carry):
        g = 2 * gp
        scores(g + 1, 1)
        finish(g, 0)
        scores(jnp.minimum(g + 2, n_groups - 1), 0)
        finish(g + 1, 1)
        return carry

    lax.fori_loop(0, n_groups // 2, body, 0, unroll=4)


def _attention(qkv, bias, batch, seq):
    m = qkv.shape[0]
    rows = seq // GRID_W
    assert rows >= WIN_ROWS and rows % (2 * ATTN_ROWS) == 0 and qkv.shape[1] == 3 * N_PAIRS * PAIR_W
    band = WIN_ROWS * GRID_W
    blk = lambda off: pl.BlockSpec((seq, PAIR_W), lambda b, p: (b, off + p))
    s_shape = (2, ATTN_ROWS, PAIR_W, band)
    est = 2 * (4 * _nbytes((seq, PAIR_W), jnp.bfloat16) + _nbytes((WIN_ROWS, PAIR_W, band), jnp.float32))
    est += 2 * _nbytes(s_shape, jnp.float32)
    return pl.pallas_call(
        functools.partial(_attn_kernel, rows=rows),
        grid=(batch, N_PAIRS),
        in_specs=[blk(0), blk(N_PAIRS), blk(2 * N_PAIRS),
                  pl.BlockSpec((1, WIN_ROWS, PAIR_W, band), lambda b, p: (p, 0, 0, 0))],
        out_specs=pl.BlockSpec((seq, PAIR_W), lambda b, p: (b, p)),
        out_shape=jax.ShapeDtypeStruct((m, N_PAIRS * PAIR_W), jnp.bfloat16),
        scratch_shapes=[pltpu.VMEM(s_shape, jnp.float32)],
        compiler_params=_params(("parallel", "parallel"), est),
        name="natten",
    )(qkv, qkv, qkv, bias)


def _outproj_kernel(yc_ref, ya_ref, wt_ref, wb_ref, x_ref, g1_ref, g2_ref, xo_ref, h_ref):
    sub = yc_ref.shape[0] // OUTPROJ_SUBBLOCKS
    for b in range(OUTPROJ_SUBBLOCKS):
        rows = pl.ds(b * sub, sub)
        mix = jnp.dot(yc_ref[rows, :], wt_ref[...], preferred_element_type=jnp.float32)
        mix = mix + jnp.dot(ya_ref[rows, :], wb_ref[...], preferred_element_type=jnp.float32)
        xn = x_ref[rows, :] + _rms(mix, g1_ref[...])
        xo_ref[rows, :] = xn
        h_ref[rows, :] = _rms(xn, g2_ref[...]).astype(h_ref.dtype)


def _outproj(yc, ya, w_all, layer, x2, g_post, g_next, tm=512):
    m, d = x2.shape
    c = yc.shape[1]
    est = 2 * (2 * _nbytes((tm, c), yc.dtype) + 2 * _nbytes((c, d), w_all.dtype)
               + 2 * _nbytes((tm, d), jnp.float32) + _nbytes((tm, d), jnp.bfloat16))
    est += 2 * _nbytes((tm, d), jnp.float32)
    act = lambda width: pl.BlockSpec((tm, width), lambda i: (i, 0))
    wspec = lambda half: pl.BlockSpec((None, c, d), lambda i: (layer, half, 0))
    vec = pl.BlockSpec((1, d), lambda i: (0, 0))
    return pl.pallas_call(
        _outproj_kernel,
        grid=(m // tm,),
        in_specs=[act(c), act(c), wspec(0), wspec(1), act(d), vec, vec],
        out_specs=[act(d), act(d)],
        out_shape=[jax.ShapeDtypeStruct((m, d), jnp.float32),
                   jax.ShapeDtypeStruct((m, d), jnp.bfloat16)],
        compiler_params=_params(("parallel",), est),
        name="outproj",
    )(yc, ya, w_all, w_all, x2, g_post, g_next)


def _mlp_kernel(h_ref, wu_ref, wd_ref, x_ref, g1_ref, g2_ref, xo_ref, *rest, emit_next):
    f = pl.program_id(1)

    @pl.when(f == 0)
    def _():
        xo_ref[...] = jnp.zeros_like(xo_ref)

    hid = jnp.dot(h_ref[...], wu_ref[...], preferred_element_type=jnp.float32)
    hid = jnp.square(jnp.maximum(hid, 0.0)).astype(wd_ref.dtype)
    xo_ref[...] += jnp.dot(hid, wd_ref[...], preferred_element_type=jnp.float32)

    @pl.when(f == pl.num_programs(1) - 1)
    def _():
        xn = x_ref[...] + _rms(xo_ref[...], g1_ref[...])
        xo_ref[...] = xn
        if emit_next:
            (hn_ref,) = rest
            hn_ref[...] = _rms(xn, g2_ref[...]).astype(hn_ref.dtype)


def _mlp(h, w_up_all, w_down_all, layer, x2, g_post, g_next, emit_next, tm=512, tf=1024):
    m, d = x2.shape
    est = 2 * (_nbytes((tm, d), h.dtype) + _nbytes((tm, d), jnp.float32)
               + _nbytes((d, tf), w_up_all.dtype) + _nbytes((tf, d), w_down_all.dtype)
               + _nbytes((tm, d), jnp.float32) + _nbytes((tm, d), jnp.bfloat16) * emit_next)
    est += 2 * _nbytes((tm, tf), jnp.float32) + _nbytes((tm, d), jnp.float32)
    act = pl.BlockSpec((tm, d), lambda i, f: (i, 0))
    vec = pl.BlockSpec((1, d), lambda i, f: (0, 0))
    out_specs = [act]
    out_shape = [jax.ShapeDtypeStruct((m, d), jnp.float32)]
    if emit_next:
        out_specs.append(act)
        out_shape.append(jax.ShapeDtypeStruct((m, d), jnp.bfloat16))
    res = pl.pallas_call(
        functools.partial(_mlp_kernel, emit_next=emit_next),
        grid=(m // tm, w_up_all.shape[2] // tf),
        in_specs=[act,
                  pl.BlockSpec((None, d, tf), lambda i, f: (layer, 0, f)),
                  pl.BlockSpec((None, tf, d), lambda i, f: (layer, f, 0)),
                  act, vec, vec],
        out_specs=out_specs,
        out_shape=out_shape,
        compiler_params=_params(("parallel", "arbitrary"), est),
        name="mlp",
    )(h, w_up_all, w_down_all, x2, g_post, g_next)
    return (res[0], res[1]) if emit_next else (res[0], None)


def kernel(x, w_in, w_dw, b_dw, conv_ln_g, conv_ln_b, rpb, w_out, w_up, w_down,
           pre_mix_g, post_mix_g, pre_mlp_g, post_mlp_g):
    batch, seq, d = x.shape
    depth = w_in.shape[0]
    conv_w = w_dw.shape[2]
    assert seq % GRID_W == 0 and w_in.shape[2] == 2 * conv_w + 3 * NA_HEADS * HEAD_DIM
    bf16 = jnp.bfloat16
    row = lambda v: v.reshape(1, -1)
    rows3 = lambda v: v.reshape(depth, 1, -1)

    w_in_b, w_out_b, w_up_b, w_down_b = (w.astype(bf16) for w in (w_in, w_out, w_up, w_down))
    b_dw3, ln_g3, ln_b3 = rows3(b_dw), rows3(conv_ln_g), rows3(conv_ln_b)
    qkv_w = 3 * NA_HEADS * HEAD_DIM

    x2 = x.reshape(batch * seq, d)
    h = _prenorm(x2, row(pre_mix_g[0]))
    for l in range(depth):
        ag = _matmul(h, w_in_b, l, 0, 2 * conv_w, jnp.float32, name="inproj_conv")
        qkv = _matmul(h, w_in_b, l, 2 * conv_w, qkv_w, bf16, name="inproj_qkv")
        yc = _conv_group(ag, w_dw, b_dw3, ln_g3, ln_b3, l, seq)
        ya = _attention(qkv, _bias_table(rpb[l]), batch, seq)
        x2, h = _outproj(yc, ya, w_out_b, l, x2, row(post_mix_g[l]), row(pre_mlp_g[l]))
        last = l == depth - 1
        g_next = row(post_mlp_g[l] if last else pre_mix_g[l + 1])
        x2, h = _mlp(h, w_up_b, w_down_b, l, x2, row(post_mlp_g[l]), g_next, emit_next=not last)
    return x2.reshape(batch, seq, d)
```

```python
import functools

import jax
import jax.numpy as jnp
from jax import lax
from jax.experimental import pallas as pl
from jax.experimental.pallas import tpu as pltpu

LANES = 128
SUBLANES = 8
GRID_W = 64
NA_HEADS = 16
HEAD_DIM = 64
PAIR_W = 2 * HEAD_DIM
N_PAIRS = NA_HEADS // 2
CONV_K = 31
WIN_ROWS = 8
WIN_COLS = 16
RMS_EPS = 1e-6
LN_EPS = 1e-5
NEG_INF = -1e30
LOG2_E = 1.4426950408889634

CONV_HALO = 16
CONV_ROWS = 128
NORM_ROWS = 16
OUTPROJ_SUBBLOCKS = 2
ATTN_ROWS = 2

V7X_VMEM_BYTES = 64 * 1024 * 1024
VMEM_RESERVE = 6 * 1024 * 1024


def _params(semantics, est_bytes):
    limit = min(int(est_bytes) + VMEM_RESERVE, V7X_VMEM_BYTES - VMEM_RESERVE)
    return pltpu.CompilerParams(dimension_semantics=semantics, vmem_limit_bytes=limit)


def _nbytes(shape, dtype):
    n = 1
    for s in shape:
        n *= s
    return n * jnp.dtype(dtype).itemsize


def _sigmoid(x):
    return 0.5 * jnp.tanh(0.5 * x) + 0.5


def _rms(xf, g):
    y = xf * lax.rsqrt(jnp.mean(xf * xf, axis=-1, keepdims=True) + RMS_EPS)
    return y * g


def _prenorm_kernel(x_ref, g_ref, h_ref):
    h_ref[...] = _rms(x_ref[...], g_ref[...]).astype(h_ref.dtype)


def _prenorm(x2, g, tm=512):
    m, d = x2.shape
    est = 2 * (_nbytes((tm, d), jnp.float32) + _nbytes((tm, d), jnp.bfloat16))
    return pl.pallas_call(
        _prenorm_kernel,
        grid=(m // tm,),
        in_specs=[pl.BlockSpec((tm, d), lambda i: (i, 0)),
                  pl.BlockSpec((1, d), lambda i: (0, 0))],
        out_specs=pl.BlockSpec((tm, d), lambda i: (i, 0)),
        out_shape=jax.ShapeDtypeStruct((m, d), jnp.bfloat16),
        compiler_params=_params(("parallel",), est),
        name="prenorm",
    )(x2, g)


def _matmul_kernel(h_ref, w_ref, o_ref):
    o_ref[...] = jnp.dot(h_ref[...], w_ref[...],
                         preferred_element_type=jnp.float32).astype(o_ref.dtype)


def _matmul(h, w, col0, n, out_dtype, tm=2048, tn=1024, name="inproj"):
    m, k = h.shape
    assert col0 % tn == 0 and n % tn == 0 and m % tm == 0
    est = 2 * (_nbytes((tm, k), h.dtype) + _nbytes((k, tn), w.dtype) + _nbytes((tm, tn), out_dtype))
    est += _nbytes((tm, tn), jnp.float32)
    return pl.pallas_call(
        _matmul_kernel,
        grid=(m // tm, n // tn),
        in_specs=[pl.BlockSpec((tm, k), lambda i, j: (i, 0)),
                  pl.BlockSpec((k, tn), lambda i, j: (0, col0 // tn + j))],
        out_specs=pl.BlockSpec((tm, tn), lambda i, j: (i, j)),
        out_shape=jax.ShapeDtypeStruct((m, n), out_dtype),
        compiler_params=_params(("parallel", "arbitrary"), est),
        name=name,
    )(h, w)


def _conv_kernel(a_ref, g_ref, ap_ref, gp_ref, an_ref, gn_ref, w_ref, b_ref, lg_ref, lb_ref,
                 o_ref, ext_ref, y_ref, *, tt, tiles_per_seq):
    ti = lax.rem(pl.program_id(0), tiles_per_seq)
    groups = ext_ref.shape[0]

    def glu(a, g):
        return a * _sigmoid(g)

    def put(row0, nrows, val):
        for gi in range(groups):
            ext_ref[gi, pl.ds(row0, nrows), :] = val[:, gi * LANES:(gi + 1) * LANES]

    put(CONV_HALO, tt, glu(a_ref[...], g_ref[...]))
    up = glu(ap_ref[...], gp_ref[...])
    put(0, CONV_HALO, jnp.where(ti > 0, up, jnp.zeros_like(up)))
    un = glu(an_ref[...], gn_ref[...])
    put(CONV_HALO + tt, CONV_HALO, jnp.where(ti < tiles_per_seq - 1, un, jnp.zeros_like(un)))

    first_tap = CONV_HALO - CONV_K // 2
    sub_tiles = CONV_ROWS // SUBLANES

    for gi in range(groups):
        cols = slice(gi * LANES, (gi + 1) * LANES)
        taps = [jnp.broadcast_to(w_ref[k:k + 1, cols], (SUBLANES, LANES)) for k in range(CONV_K)]

        def conv_chunk(c, carry, gi=gi, cols=cols, taps=taps):
            r0 = pl.multiple_of(c * CONV_ROWS, CONV_ROWS)
            for j in range(sub_tiles):
                row = r0 + first_tap + j * SUBLANES
                acc = ext_ref[gi, pl.ds(row, SUBLANES), :] * taps[0]
                for k in range(1, CONV_K):
                    acc = acc + ext_ref[gi, pl.ds(row + k, SUBLANES), :] * taps[k]
                y_ref[pl.ds(r0 + j * SUBLANES, SUBLANES), cols] = acc
            return carry

        lax.fori_loop(0, tt // CONV_ROWS, conv_chunk, 0)

    def norm_chunk(c, carry):
        r0 = pl.multiple_of(c * NORM_ROWS, NORM_ROWS)
        y = y_ref[pl.ds(r0, NORM_ROWS), :] + b_ref[...]
        mu = jnp.mean(y, axis=-1, keepdims=True)
        yc = y - mu
        var = jnp.mean(yc * yc, axis=-1, keepdims=True)
        yn = yc * lax.rsqrt(var + LN_EPS) * lg_ref[...] + lb_ref[...]
        o_ref[pl.ds(r0, NORM_ROWS), :] = (yn * _sigmoid(yn)).astype(o_ref.dtype)
        return carry

    lax.fori_loop(0, tt // NORM_ROWS, norm_chunk, 0, unroll=8)


def _conv_group(ag, w_dw, b_dw, ln_g, ln_b, layer, seq, tt=512):
    m = ag.shape[0]
    c = w_dw.shape[2]
    assert ag.shape[1] == 2 * c and seq % tt == 0 and tt % CONV_HALO == 0
    hb = tt // CONV_HALO
    last_hb = m // CONV_HALO - 1
    main = lambda col: pl.BlockSpec((tt, c), lambda i: (i, col))
    prev = lambda col: pl.BlockSpec((CONV_HALO, c), lambda i: (jnp.maximum(i * hb - 1, 0), col))
    nxt = lambda col: pl.BlockSpec((CONV_HALO, c), lambda i: (jnp.minimum((i + 1) * hb, last_hb), col))
    vec = pl.BlockSpec((None, 1, c), lambda i: (layer, 0, 0))
    est = 2 * (2 * _nbytes((tt, c), ag.dtype) + 4 * _nbytes((CONV_HALO, c), ag.dtype)
               + _nbytes((tt, c), jnp.bfloat16)) + 3 * _nbytes((tt + 2 * CONV_HALO, c), jnp.float32)
    return pl.pallas_call(
        functools.partial(_conv_kernel, tt=tt, tiles_per_seq=seq // tt),
        grid=(m // tt,),
        in_specs=[main(0), main(1), prev(0), prev(1), nxt(0), nxt(1),
                  pl.BlockSpec((None, CONV_K, c), lambda i: (layer, 0, 0)), vec, vec, vec],
        out_specs=pl.BlockSpec((tt, c), lambda i: (i, 0)),
        out_shape=jax.ShapeDtypeStruct((m, c), jnp.bfloat16),
        scratch_shapes=[pltpu.VMEM((c // LANES, tt + 2 * CONV_HALO, LANES), jnp.float32),
                        pltpu.VMEM((tt, c), jnp.float32)],
        compiler_params=_params(("parallel",), est),
        name="conv_group",
    )(ag, ag, ag, ag, ag, ag, w_dw, b_dw, ln_g, ln_b)


def _bias_kernel(rpb_ref, o_ref, w_ref):
    p = pl.program_id(0)
    wq = lax.broadcasted_iota(jnp.int32, (GRID_W, PAIR_W), 0)
    lane = lax.broadcasted_iota(jnp.int32, (GRID_W, PAIR_W), 1)
    wk = jnp.bitwise_and(lane, GRID_W - 1)
    diff = wk - wq
    cs = jnp.clip(wq - WIN_COLS // 2, 0, GRID_W - WIN_COLS)
    valid = jnp.logical_and(wk >= cs, wk < cs + WIN_COLS)
    n_dr = 2 * WIN_ROWS - 1
    n_dc = 2 * WIN_COLS - 1
    for hh in range(2):
        base = (2 * p + hh) * (n_dr * n_dc)
        for dr in range(n_dr):
            acc = jnp.zeros((GRID_W, PAIR_W), jnp.float32)
            for v in range(n_dc):
                acc = jnp.where(diff == v - (WIN_COLS - 1), rpb_ref[base + dr * n_dc + v], acc)
            w_ref[dr] = jnp.where(valid, acc * LOG2_E, NEG_INF)
        for oi in range(WIN_ROWS):
            for jj in range(WIN_ROWS // 2):
                dr0 = 2 * jj - oi + WIN_ROWS - 1
                tile = jnp.where(lane < GRID_W, w_ref[dr0], w_ref[dr0 + 1])
                o_ref[0, oi, hh * GRID_W:(hh + 1) * GRID_W, jj * PAIR_W:(jj + 1) * PAIR_W] = tile


def _bias_table(rpb_l):
    band = WIN_ROWS * GRID_W
    est = 2 * _nbytes((WIN_ROWS, PAIR_W, band), jnp.float32) + _nbytes((16, GRID_W, PAIR_W), jnp.float32)
    return pl.pallas_call(
        _bias_kernel,
        grid=(N_PAIRS,),
        in_specs=[pl.BlockSpec(memory_space=pltpu.SMEM)],
        out_specs=pl.BlockSpec((1, WIN_ROWS, PAIR_W, band), lambda p: (p, 0, 0, 0)),
        out_shape=jax.ShapeDtypeStruct((N_PAIRS, WIN_ROWS, PAIR_W, band), jnp.float32),
        scratch_shapes=[pltpu.VMEM((2 * WIN_ROWS - 1, GRID_W, PAIR_W), jnp.float32)],
        compiler_params=_params(("parallel",), est),
        name="bias_table",
    )(rpb_l.reshape(-1))


def _attn_kernel(q_ref, k_ref, v_ref, bias_ref, o_ref, s_ref, *, rows):
    band = WIN_ROWS * GRID_W
    lo = lax.broadcasted_iota(jnp.int32, (GRID_W, PAIR_W), 1) < HEAD_DIM
    scale = HEAD_DIM ** -0.5 * LOG2_E
    n_groups = rows // ATTN_ROWS

    def offsets(g, i):
        r = g * ATTN_ROWS + i
        rs = jnp.clip(r - WIN_ROWS // 2, 0, rows - WIN_ROWS)
        return (r - rs, pl.multiple_of(r * GRID_W, GRID_W), pl.multiple_of(rs * GRID_W, GRID_W))

    def scores(g, slot):
        for i in range(ATTN_ROWS):
            oi, q0, k0 = offsets(g, i)
            q2 = q_ref[pl.ds(q0, GRID_W), :]
            zero = jnp.zeros_like(q2)
            qs = jnp.concatenate([jnp.where(lo, q2, zero), jnp.where(lo, zero, q2)], axis=0)
            s = lax.dot_general(qs, k_ref[pl.ds(k0, band), :], (((1,), (1,)), ((), ())),
                                preferred_element_type=jnp.float32)
            s_ref[slot, i] = s * scale + bias_ref[0, oi]

    def finish(g, slot):
        for i in range(ATTN_ROWS):
            _, q0, k0 = offsets(g, i)
            s = s_ref[slot, i]
            e = jnp.exp2(s - jnp.max(s, axis=-1, keepdims=True))
            den = jnp.sum(e, axis=-1, keepdims=True)
            vb = v_ref[pl.ds(k0, band), :]
            pv = jnp.dot(e.astype(vb.dtype), vb, preferred_element_type=jnp.float32) / den
            o = jnp.where(lo, pv[:GRID_W], pv[GRID_W:])
            o_ref[pl.ds(q0, GRID_W), :] = o.astype(o_ref.dtype)

    scores(0, 0)

    def body(gp, carry):
        g = 2 * gp
        scores(g + 1, 1)
        finish(g, 0)
        scores(jnp.minimum(g + 2, n_groups - 1), 0)
        finish(g + 1, 1)
        return carry

    lax.fori_loop(0, n_groups // 2, body, 0, unroll=4)


def _attention(qkv, bias, batch, seq):
    m = qkv.shape[0]
    rows = seq // GRID_W
    assert rows >= WIN_ROWS and rows % (2 * ATTN_ROWS) == 0 and qkv.shape[1] == 3 * N_PAIRS * PAIR_W
    band = WIN_ROWS * GRID_W
    blk = lambda off: pl.BlockSpec((seq, PAIR_W), lambda b, p: (b, off + p))
    s_shape = (2, ATTN_ROWS, PAIR_W, band)
    est = 2 * (4 * _nbytes((seq, PAIR_W), jnp.bfloat16) + _nbytes((WIN_ROWS, PAIR_W, band), jnp.float32))
    est += 2 * _nbytes(s_shape, jnp.float32)
    return pl.pallas_call(
        functools.partial(_attn_kernel, rows=rows),
        grid=(batch, N_PAIRS),
        in_specs=[blk(0), blk(N_PAIRS), blk(2 * N_PAIRS),
                  pl.BlockSpec((1, WIN_ROWS, PAIR_W, band), lambda b, p: (p, 0, 0, 0))],
        out_specs=pl.BlockSpec((seq, PAIR_W), lambda b, p: (b, p)),
        out_shape=jax.ShapeDtypeStruct((m, N_PAIRS * PAIR_W), jnp.bfloat16),
        scratch_shapes=[pltpu.VMEM(s_shape, jnp.float32)],
        compiler_params=_params(("parallel", "parallel"), est),
        name="natten",
    )(qkv, qkv, qkv, bias)


def _outproj_kernel(yc_ref, ya_ref, wt_ref, wb_ref, x_ref, g1_ref, g2_ref, xo_ref, h_ref):
    sub = yc_ref.shape[0] // OUTPROJ_SUBBLOCKS
    for b in range(OUTPROJ_SUBBLOCKS):
        rows = pl.ds(b * sub, sub)
        mix = jnp.dot(yc_ref[rows, :], wt_ref[...], preferred_element_type=jnp.float32)
        mix = mix + jnp.dot(ya_ref[rows, :], wb_ref[...], preferred_element_type=jnp.float32)
        xn = x_ref[rows, :] + _rms(mix, g1_ref[...])
        xo_ref[rows, :] = xn
        h_ref[rows, :] = _rms(xn, g2_ref[...]).astype(h_ref.dtype)


def _outproj(yc, ya, w, x2, g_post, g_next, tm=512):
    m, d = x2.shape
    c = yc.shape[1]
    est = 2 * (2 * _nbytes((tm, c), yc.dtype) + 2 * _nbytes((c, d), w.dtype)
               + 2 * _nbytes((tm, d), jnp.float32) + _nbytes((tm, d), jnp.bfloat16))
    est += 2 * _nbytes((tm, d), jnp.float32)
    act = lambda width: pl.BlockSpec((tm, width), lambda i: (i, 0))
    wspec = lambda half: pl.BlockSpec((c, d), lambda i: (half, 0))
    vec = pl.BlockSpec((1, d), lambda i: (0, 0))
    return pl.pallas_call(
        _outproj_kernel,
        grid=(m // tm,),
        in_specs=[act(c), act(c), wspec(0), wspec(1), act(d), vec, vec],
        out_specs=[act(d), act(d)],
        out_shape=[jax.ShapeDtypeStruct((m, d), jnp.float32),
                   jax.ShapeDtypeStruct((m, d), jnp.bfloat16)],
        compiler_params=_params(("parallel",), est),
        name="outproj",
    )(yc, ya, w, w, x2, g_post, g_next)


def _mlp_kernel(h_ref, wu_ref, wd_ref, x_ref, g1_ref, g2_ref, *refs, n_next):
    cast_in, xo_ref, rest = refs[:n_next], refs[n_next], refs[n_next + 1:]
    f = pl.program_id(1)

    @pl.when(f == 0)
    def _():
        xo_ref[...] = jnp.zeros_like(xo_ref)

    hid = jnp.dot(h_ref[...], wu_ref[...], preferred_element_type=jnp.float32)
    hid = jnp.square(jnp.maximum(hid, 0.0)).astype(wd_ref.dtype)
    xo_ref[...] += jnp.dot(hid, wd_ref[...], preferred_element_type=jnp.float32)

    for src_ref, dst_ref in zip(cast_in, rest[1:]):
        dst_ref[...] = src_ref[...].astype(dst_ref.dtype)

    @pl.when(f == pl.num_programs(1) - 1)
    def _():
        xn = x_ref[...] + _rms(xo_ref[...], g1_ref[...])
        xo_ref[...] = xn
        if rest:
            rest[0][...] = _rms(xn, g2_ref[...]).astype(rest[0].dtype)


def _cast_block(shape, n_steps):
    r, c = shape
    for split in (1, 2, 4, 8):
        if (r * split) % n_steps == 0 and c % split == 0:
            br, bc = r * split // n_steps, c // split
            if br % (2 * SUBLANES) == 0 and bc % LANES == 0:
                return br, bc, split
    raise ValueError(f"no cast tiling for {shape} over {n_steps} steps")


def _mlp(h, w_up, w_down, x2, g_post, g_next, next_weights, next_layer, tm=512, tf=1024):
    m, d = x2.shape
    nf = w_up.shape[1] // tf
    n_steps = (m // tm) * nf
    emit_next = bool(next_weights)
    est = 2 * (_nbytes((tm, d), h.dtype) + _nbytes((tm, d), jnp.float32)
               + _nbytes((d, tf), w_up.dtype) + _nbytes((tf, d), w_down.dtype)
               + _nbytes((tm, d), jnp.float32) + _nbytes((tm, d), jnp.bfloat16) * emit_next)
    est += 2 * _nbytes((tm, tf), jnp.float32) + _nbytes((tm, d), jnp.float32)
    act = pl.BlockSpec((tm, d), lambda i, f: (i, 0))
    vec = pl.BlockSpec((1, d), lambda i, f: (0, 0))
    in_specs = [act,
                pl.BlockSpec((d, tf), lambda i, f: (0, f)),
                pl.BlockSpec((tf, d), lambda i, f: (f, 0)),
                act, vec, vec]
    out_specs = [act]
    out_shape = [jax.ShapeDtypeStruct((m, d), jnp.float32)]
    if emit_next:
        out_specs.append(act)
        out_shape.append(jax.ShapeDtypeStruct((m, d), jnp.bfloat16))
    cast_in_specs, cast_out_specs = [], []
    for w in next_weights:
        br, bc, split = _cast_block(w.shape[1:], n_steps)
        cast_in_specs.append(pl.BlockSpec(
            (None, br, bc),
            lambda i, f, split=split: (next_layer, (i * nf + f) // split, (i * nf + f) % split)))
        cast_out_specs.append(pl.BlockSpec(
            (br, bc), lambda i, f, split=split: ((i * nf + f) // split, (i * nf + f) % split)))
        out_shape.append(jax.ShapeDtypeStruct(w.shape[1:], w_up.dtype))
        est += 2 * (_nbytes((br, bc), w.dtype) + _nbytes((br, bc), w_up.dtype))
    res = pl.pallas_call(
        functools.partial(_mlp_kernel, n_next=len(next_weights)),
        grid=(m // tm, nf),
        in_specs=in_specs + cast_in_specs,
        out_specs=out_specs + cast_out_specs,
        out_shape=out_shape,
        compiler_params=_params(("parallel", "arbitrary"), est),
        name="mlp",
    )(h, w_up, w_down, x2, g_post, g_next, *next_weights)
    return res[0], (res[1] if emit_next else None), tuple(res[2:])


def kernel(x, w_in, w_dw, b_dw, conv_ln_g, conv_ln_b, rpb, w_out, w_up, w_down,
           pre_mix_g, post_mix_g, pre_mlp_g, post_mlp_g):
    batch, seq, d = x.shape
    depth = w_in.shape[0]
    conv_w = w_dw.shape[2]
    assert seq % GRID_W == 0 and w_in.shape[2] == 2 * conv_w + 3 * NA_HEADS * HEAD_DIM
    bf16 = jnp.bfloat16
    row = lambda v: v.reshape(1, -1)
    rows3 = lambda v: v.reshape(depth, 1, -1)

    b_dw3, ln_g3, ln_b3 = rows3(b_dw), rows3(conv_ln_g), rows3(conv_ln_b)
    qkv_w = 3 * NA_HEADS * HEAD_DIM

    mxu_weights = (w_in, w_out, w_up, w_down)
    w_in_b, w_out_b, w_up_b, w_down_b = (w[0].astype(bf16) for w in mxu_weights)

    x2 = x.reshape(batch * seq, d)
    h = _prenorm(x2, row(pre_mix_g[0]))
    for l in range(depth):
        ag = _matmul(h, w_in_b, 0, 2 * conv_w, jnp.float32, name="inproj_conv")
        qkv = _matmul(h, w_in_b, 2 * conv_w, qkv_w, bf16, name="inproj_qkv")
        yc = _conv_group(ag, w_dw, b_dw3, ln_g3, ln_b3, l, seq)
        ya = _attention(qkv, _bias_table(rpb[l]), batch, seq)
        x2, h = _outproj(yc, ya, w_out_b, x2, row(post_mix_g[l]), row(pre_mlp_g[l]))
        last = l == depth - 1
        g_next = row(post_mlp_g[l] if last else pre_mix_g[l + 1])
        x2, h, nxt = _mlp(h, w_up_b, w_down_b, x2, row(post_mlp_g[l]), g_next,
                          () if last else mxu_weights, l + 1)
        if not last:
            w_in_b, w_out_b, w_up_b, w_down_b = nxt
    return x2.reshape(batch, seq, d)
```

```python
import functools

import jax
import jax.numpy as jnp
from jax import lax
from jax.experimental import pallas as pl
from jax.experimental.pallas import tpu as pltpu

LANES = 128
SUBLANES = 8
GRID_W = 64
NA_HEADS = 16
HEAD_DIM = 64
PAIR_W = 2 * HEAD_DIM
N_PAIRS = NA_HEADS // 2
CONV_K = 31
WIN_ROWS = 8
WIN_COLS = 16
RMS_EPS = 1e-6
LN_EPS = 1e-5
NEG_INF = -1e30
LOG2_E = 1.4426950408889634

CONV_HALO = 16
CONV_ROWS = 128
NORM_ROWS = 16
OUTPROJ_SUBBLOCKS = 2
ATTN_ROWS = 2

V7X_VMEM_BYTES = 64 * 1024 * 1024
VMEM_RESERVE = 6 * 1024 * 1024


def _params(semantics, est_bytes):
    limit = min(int(est_bytes) + VMEM_RESERVE, V7X_VMEM_BYTES - VMEM_RESERVE)
    return pltpu.CompilerParams(dimension_semantics=semantics, vmem_limit_bytes=limit)


def _nbytes(shape, dtype):
    n = 1
    for s in shape:
        n *= s
    return n * jnp.dtype(dtype).itemsize


def _sigmoid(x):
    return 0.5 * jnp.tanh(0.5 * x) + 0.5


def _rms(xf, g):
    y = xf * lax.rsqrt(jnp.mean(xf * xf, axis=-1, keepdims=True) + RMS_EPS)
    return y * g


def _prenorm_kernel(x_ref, g_ref, h_ref):
    h_ref[...] = _rms(x_ref[...], g_ref[...]).astype(h_ref.dtype)


def _prenorm(x2, g, tm=512):
    m, d = x2.shape
    est = 2 * (_nbytes((tm, d), jnp.float32) + _nbytes((tm, d), jnp.bfloat16))
    return pl.pallas_call(
        _prenorm_kernel,
        grid=(m // tm,),
        in_specs=[pl.BlockSpec((tm, d), lambda i: (i, 0)),
                  pl.BlockSpec((1, d), lambda i: (0, 0))],
        out_specs=pl.BlockSpec((tm, d), lambda i: (i, 0)),
        out_shape=jax.ShapeDtypeStruct((m, d), jnp.bfloat16),
        compiler_params=_params(("parallel",), est),
        name="prenorm",
    )(x2, g)


def _matmul_kernel(h_ref, w_ref, o_ref):
    o_ref[...] = jnp.dot(h_ref[...], w_ref[...],
                         preferred_element_type=jnp.float32).astype(o_ref.dtype)


def _matmul(h, w, col0, n, out_dtype, tm=2048, tn=1024, name="inproj"):
    m, k = h.shape
    assert col0 % tn == 0 and n % tn == 0 and m % tm == 0
    est = 2 * (_nbytes((tm, k), h.dtype) + _nbytes((k, tn), w.dtype) + _nbytes((tm, tn), out_dtype))
    est += _nbytes((tm, tn), jnp.float32)
    return pl.pallas_call(
        _matmul_kernel,
        grid=(m // tm, n // tn),
        in_specs=[pl.BlockSpec((tm, k), lambda i, j: (i, 0)),
                  pl.BlockSpec((k, tn), lambda i, j: (0, col0 // tn + j))],
        out_specs=pl.BlockSpec((tm, tn), lambda i, j: (i, j)),
        out_shape=jax.ShapeDtypeStruct((m, n), out_dtype),
        compiler_params=_params(("parallel", "arbitrary"), est),
        name=name,
    )(h, w)


def _glu_matmul_kernel(h_ref, wa_ref, wg_ref, o_ref):
    h = h_ref[...]
    a = jnp.dot(h, wa_ref[...], preferred_element_type=jnp.float32)
    gate = jnp.dot(h, wg_ref[...], preferred_element_type=jnp.float32)
    o_ref[...] = a * _sigmoid(gate)


def _glu_matmul(h, w, c, tm=2048, tn=512):
    m, k = h.shape
    assert c % tn == 0 and m % tm == 0
    est = 2 * (_nbytes((tm, k), h.dtype) + 2 * _nbytes((k, tn), w.dtype) + _nbytes((tm, tn), jnp.float32))
    est += 4 * _nbytes((tm, tn), jnp.float32)
    return pl.pallas_call(
        _glu_matmul_kernel,
        grid=(m // tm, c // tn),
        in_specs=[pl.BlockSpec((tm, k), lambda i, j: (i, 0)),
                  pl.BlockSpec((k, tn), lambda i, j: (0, j)),
                  pl.BlockSpec((k, tn), lambda i, j: (0, c // tn + j))],
        out_specs=pl.BlockSpec((tm, tn), lambda i, j: (i, j)),
        out_shape=jax.ShapeDtypeStruct((m, c), jnp.float32),
        compiler_params=_params(("parallel", "arbitrary"), est),
        name="inproj_glu",
    )(h, w, w)


def _conv_kernel(u_ref, up_ref, un_ref, w_ref, b_ref, lg_ref, lb_ref,
                 o_ref, ext_ref, y_ref, *, tt, tiles_per_seq):
    ti = lax.rem(pl.program_id(0), tiles_per_seq)
    groups = ext_ref.shape[0]

    def put(row0, nrows, val):
        for gi in range(groups):
            ext_ref[gi, pl.ds(row0, nrows), :] = val[:, gi * LANES:(gi + 1) * LANES]

    put(CONV_HALO, tt, u_ref[...])
    up = up_ref[...]
    put(0, CONV_HALO, jnp.where(ti > 0, up, jnp.zeros_like(up)))
    un = un_ref[...]
    put(CONV_HALO + tt, CONV_HALO, jnp.where(ti < tiles_per_seq - 1, un, jnp.zeros_like(un)))

    first_tap = CONV_HALO - CONV_K // 2
    sub_tiles = CONV_ROWS // SUBLANES

    for gi in range(groups):
        cols = slice(gi * LANES, (gi + 1) * LANES)
        taps = [jnp.broadcast_to(w_ref[k:k + 1, cols], (SUBLANES, LANES)) for k in range(CONV_K)]

        def conv_chunk(c, carry, gi=gi, cols=cols, taps=taps):
            r0 = pl.multiple_of(c * CONV_ROWS, CONV_ROWS)
            for j in range(sub_tiles):
                row = r0 + first_tap + j * SUBLANES
                acc = ext_ref[gi, pl.ds(row, SUBLANES), :] * taps[0]
                for k in range(1, CONV_K):
                    acc = acc + ext_ref[gi, pl.ds(row + k, SUBLANES), :] * taps[k]
                y_ref[pl.ds(r0 + j * SUBLANES, SUBLANES), cols] = acc
            return carry

        lax.fori_loop(0, tt // CONV_ROWS, conv_chunk, 0)

    def norm_chunk(c, carry):
        r0 = pl.multiple_of(c * NORM_ROWS, NORM_ROWS)
        y = y_ref[pl.ds(r0, NORM_ROWS), :] + b_ref[...]
        mu = jnp.mean(y, axis=-1, keepdims=True)
        yc = y - mu
        var = jnp.mean(yc * yc, axis=-1, keepdims=True)
        yn = yc * lax.rsqrt(var + LN_EPS) * lg_ref[...] + lb_ref[...]
        o_ref[pl.ds(r0, NORM_ROWS), :] = (yn * _sigmoid(yn)).astype(o_ref.dtype)
        return carry

    lax.fori_loop(0, tt // NORM_ROWS, norm_chunk, 0, unroll=8)


def _conv_group(u, w_dw, b_dw, ln_g, ln_b, layer, seq, tt=512):
    m, c = u.shape
    assert w_dw.shape[2] == c and seq % tt == 0 and tt % CONV_HALO == 0
    hb = tt // CONV_HALO
    last_hb = m // CONV_HALO - 1
    main = pl.BlockSpec((tt, c), lambda i: (i, 0))
    prev = pl.BlockSpec((CONV_HALO, c), lambda i: (jnp.maximum(i * hb - 1, 0), 0))
    nxt = pl.BlockSpec((CONV_HALO, c), lambda i: (jnp.minimum((i + 1) * hb, last_hb), 0))
    vec = pl.BlockSpec((None, 1, c), lambda i: (layer, 0, 0))
    est = 2 * (_nbytes((tt, c), u.dtype) + 2 * _nbytes((CONV_HALO, c), u.dtype)
               + _nbytes((tt, c), jnp.bfloat16)) + 3 * _nbytes((tt + 2 * CONV_HALO, c), jnp.float32)
    return pl.pallas_call(
        functools.partial(_conv_kernel, tt=tt, tiles_per_seq=seq // tt),
        grid=(m // tt,),
        in_specs=[main, prev, nxt,
                  pl.BlockSpec((None, CONV_K, c), lambda i: (layer, 0, 0)), vec, vec, vec],
        out_specs=pl.BlockSpec((tt, c), lambda i: (i, 0)),
        out_shape=jax.ShapeDtypeStruct((m, c), jnp.bfloat16),
        scratch_shapes=[pltpu.VMEM((c // LANES, tt + 2 * CONV_HALO, LANES), jnp.float32),
                        pltpu.VMEM((tt, c), jnp.float32)],
        compiler_params=_params(("parallel",), est),
        name="conv_group",
    )(u, u, u, w_dw, b_dw, ln_g, ln_b)


def _bias_kernel(rpb_ref, o_ref, w_ref):
    p = pl.program_id(0)
    wq = lax.broadcasted_iota(jnp.int32, (GRID_W, PAIR_W), 0)
    lane = lax.broadcasted_iota(jnp.int32, (GRID_W, PAIR_W), 1)
    wk = jnp.bitwise_and(lane, GRID_W - 1)
    diff = wk - wq
    cs = jnp.clip(wq - WIN_COLS // 2, 0, GRID_W - WIN_COLS)
    valid = jnp.logical_and(wk >= cs, wk < cs + WIN_COLS)
    n_dr = 2 * WIN_ROWS - 1
    n_dc = 2 * WIN_COLS - 1
    for hh in range(2):
        base = (2 * p + hh) * (n_dr * n_dc)
        for dr in range(n_dr):
            acc = jnp.zeros((GRID_W, PAIR_W), jnp.float32)
            for v in range(n_dc):
                acc = jnp.where(diff == v - (WIN_COLS - 1), rpb_ref[base + dr * n_dc + v], acc)
            w_ref[dr] = jnp.where(valid, acc * LOG2_E, NEG_INF)
        for oi in range(WIN_ROWS):
            for jj in range(WIN_ROWS // 2):
                dr0 = 2 * jj - oi + WIN_ROWS - 1
                tile = jnp.where(lane < GRID_W, w_ref[dr0], w_ref[dr0 + 1])
                o_ref[0, oi, hh * GRID_W:(hh + 1) * GRID_W, jj * PAIR_W:(jj + 1) * PAIR_W] = tile


def _bias_table(rpb_l):
    band = WIN_ROWS * GRID_W
    est = 2 * _nbytes((WIN_ROWS, PAIR_W, band), jnp.float32) + _nbytes((16, GRID_W, PAIR_W), jnp.float32)
    return pl.pallas_call(
        _bias_kernel,
        grid=(N_PAIRS,),
        in_specs=[pl.BlockSpec(memory_space=pltpu.SMEM)],
        out_specs=pl.BlockSpec((1, WIN_ROWS, PAIR_W, band), lambda p: (p, 0, 0, 0)),
        out_shape=jax.ShapeDtypeStruct((N_PAIRS, WIN_ROWS, PAIR_W, band), jnp.float32),
        scratch_shapes=[pltpu.VMEM((2 * WIN_ROWS - 1, GRID_W, PAIR_W), jnp.float32)],
        compiler_params=_params(("parallel",), est),
        name="bias_table",
    )(rpb_l.reshape(-1))


def _attn_kernel(q_ref, k_ref, v_ref, bias_ref, *refs, rows, n_next):
    o_ref, s_ref = refs[n_next], refs[-1]
    for src_ref, dst_ref in zip(refs[:n_next], refs[n_next + 1:-1]):
        dst_ref[...] = src_ref[...].astype(dst_ref.dtype)

    band = WIN_ROWS * GRID_W
    lo = lax.broadcasted_iota(jnp.int32, (GRID_W, PAIR_W), 1) < HEAD_DIM
    scale = HEAD_DIM ** -0.5 * LOG2_E
    n_groups = rows // ATTN_ROWS

    def offsets(g, i):
        r = g * ATTN_ROWS + i
        rs = jnp.clip(r - WIN_ROWS // 2, 0, rows - WIN_ROWS)
        return (r - rs, pl.multiple_of(r * GRID_W, GRID_W), pl.multiple_of(rs * GRID_W, GRID_W))

    def scores(g, slot):
        for i in range(ATTN_ROWS):
            oi, q0, k0 = offsets(g, i)
            q2 = q_ref[pl.ds(q0, GRID_W), :]
            zero = jnp.zeros_like(q2)
            qs = jnp.concatenate([jnp.where(lo, q2, zero), jnp.where(lo, zero, q2)], axis=0)
            s = lax.dot_general(qs, k_ref[pl.ds(k0, band), :], (((1,), (1,)), ((), ())),
                                preferred_element_type=jnp.float32)
            s_ref[slot, i] = s * scale + bias_ref[0, oi]

    def finish(g, slot):
        for i in range(ATTN_ROWS):
            _, q0, k0 = offsets(g, i)
            s = s_ref[slot, i]
            e = jnp.exp2(s - jnp.max(s, axis=-1, keepdims=True))
            den = jnp.sum(e, axis=-1, keepdims=True)
            vb = v_ref[pl.ds(k0, band), :]
            pv = jnp.dot(e.astype(vb.dtype), vb, preferred_element_type=jnp.float32) / den
            o = jnp.where(lo, pv[:GRID_W], pv[GRID_W:])
            o_ref[pl.ds(q0, GRID_W), :] = o.astype(o_ref.dtype)

    scores(0, 0)

    def body(gp, carry):
        g = 2 * gp
        scores(g + 1, 1)
        finish(g, 0)
        scores(jnp.minimum(g + 2, n_groups - 1), 0)
        finish(g + 1, 1)
        return carry

    lax.fori_loop(0, n_groups // 2, body, 0, unroll=4)


def _cast_block(shape, n_steps):
    r, c = shape
    for split in (1, 2, 4, 8):
        if (r * split) % n_steps == 0 and c % split == 0:
            br, bc = r * split // n_steps, c // split
            if br % (2 * SUBLANES) == 0 and bc % LANES == 0:
                return br, bc, split
    raise ValueError(f"no cast tiling for {shape} over {n_steps} steps")


def _attention(qkv, bias, batch, seq, next_weights, next_layer):
    m = qkv.shape[0]
    rows = seq // GRID_W
    assert rows >= WIN_ROWS and rows % (2 * ATTN_ROWS) == 0 and qkv.shape[1] == 3 * N_PAIRS * PAIR_W
    band = WIN_ROWS * GRID_W
    blk = lambda off: pl.BlockSpec((seq, PAIR_W), lambda b, p: (b, off + p))
    s_shape = (2, ATTN_ROWS, PAIR_W, band)
    est = 2 * (4 * _nbytes((seq, PAIR_W), jnp.bfloat16) + _nbytes((WIN_ROWS, PAIR_W, band), jnp.float32))
    est += 2 * _nbytes(s_shape, jnp.float32)
    cast_in_specs, cast_out_specs, cast_shapes = [], [], []
    for w in next_weights:
        br, bc, split = _cast_block(w.shape[1:], batch * N_PAIRS)
        step = lambda b, p: b * N_PAIRS + p
        cast_in_specs.append(pl.BlockSpec(
            (None, br, bc), lambda b, p, split=split: (next_layer, step(b, p) // split, step(b, p) % split)))
        cast_out_specs.append(pl.BlockSpec(
            (br, bc), lambda b, p, split=split: (step(b, p) // split, step(b, p) % split)))
        cast_shapes.append(jax.ShapeDtypeStruct(w.shape[1:], jnp.bfloat16))
        est += 2 * (_nbytes((br, bc), w.dtype) + _nbytes((br, bc), jnp.bfloat16))
    res = pl.pallas_call(
        functools.partial(_attn_kernel, rows=rows, n_next=len(next_weights)),
        grid=(batch, N_PAIRS),
        in_specs=[blk(0), blk(N_PAIRS), blk(2 * N_PAIRS),
                  pl.BlockSpec((1, WIN_ROWS, PAIR_W, band), lambda b, p: (p, 0, 0, 0))] + cast_in_specs,
        out_specs=[pl.BlockSpec((seq, PAIR_W), lambda b, p: (b, p))] + cast_out_specs,
        out_shape=[jax.ShapeDtypeStruct((m, N_PAIRS * PAIR_W), jnp.bfloat16)] + cast_shapes,
        scratch_shapes=[pltpu.VMEM(s_shape, jnp.float32)],
        compiler_params=_params(("parallel", "parallel"), est),
        name="natten",
    )(qkv, qkv, qkv, bias, *next_weights)
    return res[0], tuple(res[1:])


def _outproj_kernel(yc_ref, ya_ref, wt_ref, wb_ref, x_ref, g1_ref, g2_ref, xo_ref, h_ref):
    sub = yc_ref.shape[0] // OUTPROJ_SUBBLOCKS
    for b in range(OUTPROJ_SUBBLOCKS):
        rows = pl.ds(b * sub, sub)
        mix = jnp.dot(yc_ref[rows, :], wt_ref[...], preferred_element_type=jnp.float32)
        mix = mix + jnp.dot(ya_ref[rows, :], wb_ref[...], preferred_element_type=jnp.float32)
        xn = x_ref[rows, :] + _rms(mix, g1_ref[...])
        xo_ref[rows, :] = xn
        h_ref[rows, :] = _rms(xn, g2_ref[...]).astype(h_ref.dtype)


def _outproj(yc, ya, w, x2, g_post, g_next, tm=512):
    m, d = x2.shape
    c = yc.shape[1]
    est = 2 * (2 * _nbytes((tm, c), yc.dtype) + 2 * _nbytes((c, d), w.dtype)
               + 2 * _nbytes((tm, d), jnp.float32) + _nbytes((tm, d), jnp.bfloat16))
    est += 2 * _nbytes((tm, d), jnp.float32)
    act = lambda width: pl.BlockSpec((tm, width), lambda i: (i, 0))
    wspec = lambda half: pl.BlockSpec((c, d), lambda i: (half, 0))
    vec = pl.BlockSpec((1, d), lambda i: (0, 0))
    return pl.pallas_call(
        _outproj_kernel,
        grid=(m // tm,),
        in_specs=[act(c), act(c), wspec(0), wspec(1), act(d), vec, vec],
        out_specs=[act(d), act(d)],
        out_shape=[jax.ShapeDtypeStruct((m, d), jnp.float32),
                   jax.ShapeDtypeStruct((m, d), jnp.bfloat16)],
        compiler_params=_params(("parallel",), est),
        name="outproj",
    )(yc, ya, w, w, x2, g_post, g_next)


def _mlp_kernel(h_ref, wu_ref, wd_ref, x_ref, g1_ref, g2_ref, xo_ref, *rest):
    f = pl.program_id(1)

    @pl.when(f == 0)
    def _():
        xo_ref[...] = jnp.zeros_like(xo_ref)

    hid = jnp.dot(h_ref[...], wu_ref[...], preferred_element_type=jnp.float32)
    hid = jnp.square(jnp.maximum(hid, 0.0)).astype(wd_ref.dtype)
    xo_ref[...] += jnp.dot(hid, wd_ref[...], preferred_element_type=jnp.float32)

    @pl.when(f == pl.num_programs(1) - 1)
    def _():
        xn = x_ref[...] + _rms(xo_ref[...], g1_ref[...])
        xo_ref[...] = xn
        if rest:
            rest[0][...] = _rms(xn, g2_ref[...]).astype(rest[0].dtype)


def _mlp(h, w_up, w_down, x2, g_post, g_next, emit_next, tm=512, tf=1024):
    m, d = x2.shape
    nf = w_up.shape[1] // tf
    est = 2 * (_nbytes((tm, d), h.dtype) + _nbytes((tm, d), jnp.float32)
               + _nbytes((d, tf), w_up.dtype) + _nbytes((tf, d), w_down.dtype)
               + _nbytes((tm, d), jnp.float32) + _nbytes((tm, d), jnp.bfloat16) * emit_next)
    est += 2 * _nbytes((tm, tf), jnp.float32) + _nbytes((tm, d), jnp.float32)
    act = pl.BlockSpec((tm, d), lambda i, f: (i, 0))
    vec = pl.BlockSpec((1, d), lambda i, f: (0, 0))
    in_specs = [act,
                pl.BlockSpec((d, tf), lambda i, f: (0, f)),
                pl.BlockSpec((tf, d), lambda i, f: (f, 0)),
                act, vec, vec]
    out_specs = [act]
    out_shape = [jax.ShapeDtypeStruct((m, d), jnp.float32)]
    if emit_next:
        out_specs.append(act)
        out_shape.append(jax.ShapeDtypeStruct((m, d), jnp.bfloat16))
    res = pl.pallas_call(
        _mlp_kernel,
        grid=(m // tm, nf),
        in_specs=in_specs,
        out_specs=out_specs,
        out_shape=out_shape,
        compiler_params=_params(("parallel", "arbitrary"), est),
        name="mlp",
    )(h, w_up, w_down, x2, g_post, g_next)
    return res[0], (res[1] if emit_next else None)


def kernel(x, w_in, w_dw, b_dw, conv_ln_g, conv_ln_b, rpb, w_out, w_up, w_down,
           pre_mix_g, post_mix_g, pre_mlp_g, post_mlp_g):
    batch, seq, d = x.shape
    depth = w_in.shape[0]
    conv_w = w_dw.shape[2]
    assert seq % GRID_W == 0 and w_in.shape[2] == 2 * conv_w + 3 * NA_HEADS * HEAD_DIM
    bf16 = jnp.bfloat16
    row = lambda v: v.reshape(1, -1)
    rows3 = lambda v: v.reshape(depth, 1, -1)

    b_dw3, ln_g3, ln_b3 = rows3(b_dw), rows3(conv_ln_g), rows3(conv_ln_b)
    qkv_w = 3 * NA_HEADS * HEAD_DIM

    mxu_weights = (w_in, w_out, w_up, w_down)
    w_in_b, w_out_b, w_up_b, w_down_b = (w[0].astype(bf16) for w in mxu_weights)

    x2 = x.reshape(batch * seq, d)
    h = _prenorm(x2, row(pre_mix_g[0]))
    for l in range(depth):
        last = l == depth - 1
        u = _glu_matmul(h, w_in_b, conv_w)
        qkv = _matmul(h, w_in_b, 2 * conv_w, qkv_w, bf16, name="inproj_qkv")
        yc = _conv_group(u, w_dw, b_dw3, ln_g3, ln_b3, l, seq)
        ya, nxt = _attention(qkv, _bias_table(rpb[l]), batch, seq, () if last else mxu_weights, l + 1)
        x2, h = _outproj(yc, ya, w_out_b, x2, row(post_mix_g[l]), row(pre_mlp_g[l]))
        g_next = row(post_mlp_g[l] if last else pre_mix_g[l + 1])
        x2, h = _mlp(h, w_up_b, w_down_b, x2, row(post_mlp_g[l]), g_next, emit_next=not last)
        if not last:
            w_in_b, w_out_b, w_up_b, w_down_b = nxt
    return x2.reshape(batch, seq, d)
```

```python
import functools

import jax
import jax.numpy as jnp
from jax import lax
from jax.experimental import pallas as pl
from jax.experimental.pallas import tpu as pltpu

LANES = 128
SUBLANES = 8
GRID_W = 64
NA_HEADS = 16
HEAD_DIM = 64
PAIR_W = 2 * HEAD_DIM
N_PAIRS = NA_HEADS // 2
CONV_K = 31
WIN_ROWS = 8
WIN_COLS = 16
RMS_EPS = 1e-6
LN_EPS = 1e-5
NEG_INF = -1e30
LOG2_E = 1.4426950408889634

CONV_HALO = 16
CONV_ROWS = 128
NORM_ROWS = 16
OUTPROJ_SUBBLOCKS = 2
ATTN_ROWS = 2

V7X_VMEM_BYTES = 64 * 1024 * 1024
VMEM_RESERVE = 6 * 1024 * 1024


def _params(semantics, est_bytes, claim_all_vmem=False):
    limit = V7X_VMEM_BYTES - VMEM_RESERVE
    if not claim_all_vmem:
        limit = min(int(est_bytes) + VMEM_RESERVE, limit)
    return pltpu.CompilerParams(dimension_semantics=semantics, vmem_limit_bytes=limit)


def _nbytes(shape, dtype):
    n = 1
    for s in shape:
        n *= s
    return n * jnp.dtype(dtype).itemsize


def _sigmoid(x):
    return 0.5 * jnp.tanh(0.5 * x) + 0.5


def _rms(xf, g):
    y = xf * lax.rsqrt(jnp.mean(xf * xf, axis=-1, keepdims=True) + RMS_EPS)
    return y * g


def _prenorm_kernel(x_ref, g_ref, h_ref):
    h_ref[...] = _rms(x_ref[...], g_ref[...]).astype(h_ref.dtype)


def _prenorm(x2, g, tm=512):
    m, d = x2.shape
    est = 2 * (_nbytes((tm, d), jnp.float32) + _nbytes((tm, d), jnp.bfloat16))
    return pl.pallas_call(
        _prenorm_kernel,
        grid=(m // tm,),
        in_specs=[pl.BlockSpec((tm, d), lambda i: (i, 0)),
                  pl.BlockSpec((1, d), lambda i: (0, 0))],
        out_specs=pl.BlockSpec((tm, d), lambda i: (i, 0)),
        out_shape=jax.ShapeDtypeStruct((m, d), jnp.bfloat16),
        compiler_params=_params(("parallel",), est),
        name="prenorm",
    )(x2, g)


def _matmul_kernel(h_ref, w_ref, o_ref):
    o_ref[...] = jnp.dot(h_ref[...], w_ref[...],
                         preferred_element_type=jnp.float32).astype(o_ref.dtype)


def _matmul(h, w, col0, n, out_dtype, tm=2048, tn=1024, name="inproj"):
    m, k = h.shape
    assert col0 % tn == 0 and n % tn == 0 and m % tm == 0
    est = 2 * (_nbytes((tm, k), h.dtype) + _nbytes((k, tn), w.dtype) + _nbytes((tm, tn), out_dtype))
    est += _nbytes((tm, tn), jnp.float32)
    return pl.pallas_call(
        _matmul_kernel,
        grid=(m // tm, n // tn),
        in_specs=[pl.BlockSpec((tm, k), lambda i, j: (i, 0)),
                  pl.BlockSpec((k, tn), lambda i, j: (0, col0 // tn + j))],
        out_specs=pl.BlockSpec((tm, tn), lambda i, j: (i, j)),
        out_shape=jax.ShapeDtypeStruct((m, n), out_dtype),
        compiler_params=_params(("parallel", "arbitrary"), est),
        name=name,
    )(h, w)


def _glu_matmul_kernel(h_ref, wa_ref, wg_ref, o_ref):
    h = h_ref[...]
    a = jnp.dot(h, wa_ref[...], preferred_element_type=jnp.float32)
    gate = jnp.dot(h, wg_ref[...], preferred_element_type=jnp.float32)
    o_ref[...] = a * _sigmoid(gate)


def _glu_matmul(h, w, c, tm=2048, tn=512):
    m, k = h.shape
    assert c % tn == 0 and m % tm == 0
    est = 2 * (_nbytes((tm, k), h.dtype) + 2 * _nbytes((k, tn), w.dtype) + _nbytes((tm, tn), jnp.float32))
    est += 4 * _nbytes((tm, tn), jnp.float32)
    return pl.pallas_call(
        _glu_matmul_kernel,
        grid=(m // tm, c // tn),
        in_specs=[pl.BlockSpec((tm, k), lambda i, j: (i, 0)),
                  pl.BlockSpec((k, tn), lambda i, j: (0, j)),
                  pl.BlockSpec((k, tn), lambda i, j: (0, c // tn + j))],
        out_specs=pl.BlockSpec((tm, tn), lambda i, j: (i, j)),
        out_shape=jax.ShapeDtypeStruct((m, c), jnp.float32),
        compiler_params=_params(("parallel", "arbitrary"), est),
        name="inproj_glu",
    )(h, w, w)


def _conv_kernel(u_ref, up_ref, un_ref, w_ref, b_ref, lg_ref, lb_ref,
                 o_ref, ext_ref, y_ref, *, tt, tiles_per_seq):
    ti = lax.rem(pl.program_id(0), tiles_per_seq)
    groups = ext_ref.shape[0]

    def put(row0, nrows, val):
        for gi in range(groups):
            ext_ref[gi, pl.ds(row0, nrows), :] = val[:, gi * LANES:(gi + 1) * LANES]

    put(CONV_HALO, tt, u_ref[...])
    up = up_ref[...]
    put(0, CONV_HALO, jnp.where(ti > 0, up, jnp.zeros_like(up)))
    un = un_ref[...]
    put(CONV_HALO + tt, CONV_HALO, jnp.where(ti < tiles_per_seq - 1, un, jnp.zeros_like(un)))

    first_tap = CONV_HALO - CONV_K // 2
    sub_tiles = CONV_ROWS // SUBLANES

    for gi in range(groups):
        cols = slice(gi * LANES, (gi + 1) * LANES)
        taps = [jnp.broadcast_to(w_ref[k:k + 1, cols], (SUBLANES, LANES)) for k in range(CONV_K)]

        def conv_chunk(c, carry, gi=gi, cols=cols, taps=taps):
            r0 = pl.multiple_of(c * CONV_ROWS, CONV_ROWS)
            for j in range(sub_tiles):
                row = r0 + first_tap + j * SUBLANES
                acc = ext_ref[gi, pl.ds(row, SUBLANES), :] * taps[0]
                for k in range(1, CONV_K):
                    acc = acc + ext_ref[gi, pl.ds(row + k, SUBLANES), :] * taps[k]
                y_ref[pl.ds(r0 + j * SUBLANES, SUBLANES), cols] = acc
            return carry

        lax.fori_loop(0, tt // CONV_ROWS, conv_chunk, 0)

    def norm_chunk(c, carry):
        r0 = pl.multiple_of(c * NORM_ROWS, NORM_ROWS)
        y = y_ref[pl.ds(r0, NORM_ROWS), :] + b_ref[...]
        mu = jnp.mean(y, axis=-1, keepdims=True)
        yc = y - mu
        var = jnp.mean(yc * yc, axis=-1, keepdims=True)
        yn = yc * lax.rsqrt(var + LN_EPS) * lg_ref[...] + lb_ref[...]
        o_ref[pl.ds(r0, NORM_ROWS), :] = (yn * _sigmoid(yn)).astype(o_ref.dtype)
        return carry

    lax.fori_loop(0, tt // NORM_ROWS, norm_chunk, 0, unroll=8)


def _conv_group(u, w_dw, b_dw, ln_g, ln_b, layer, seq, tt=512):
    m, c = u.shape
    assert w_dw.shape[2] == c and seq % tt == 0 and tt % CONV_HALO == 0
    hb = tt // CONV_HALO
    last_hb = m // CONV_HALO - 1
    main = pl.BlockSpec((tt, c), lambda i: (i, 0))
    prev = pl.BlockSpec((CONV_HALO, c), lambda i: (jnp.maximum(i * hb - 1, 0), 0))
    nxt = pl.BlockSpec((CONV_HALO, c), lambda i: (jnp.minimum((i + 1) * hb, last_hb), 0))
    vec = pl.BlockSpec((None, 1, c), lambda i: (layer, 0, 0))
    est = 2 * (_nbytes((tt, c), u.dtype) + 2 * _nbytes((CONV_HALO, c), u.dtype)
               + _nbytes((tt, c), jnp.bfloat16)) + 3 * _nbytes((tt + 2 * CONV_HALO, c), jnp.float32)
    return pl.pallas_call(
        functools.partial(_conv_kernel, tt=tt, tiles_per_seq=seq // tt),
        grid=(m // tt,),
        in_specs=[main, prev, nxt,
                  pl.BlockSpec((None, CONV_K, c), lambda i: (layer, 0, 0)), vec, vec, vec],
        out_specs=pl.BlockSpec((tt, c), lambda i: (i, 0)),
        out_shape=jax.ShapeDtypeStruct((m, c), jnp.bfloat16),
        scratch_shapes=[pltpu.VMEM((c // LANES, tt + 2 * CONV_HALO, LANES), jnp.float32),
                        pltpu.VMEM((tt, c), jnp.float32)],
        compiler_params=_params(("parallel",), est, claim_all_vmem=True),
        name="conv_group",
    )(u, u, u, w_dw, b_dw, ln_g, ln_b)


def _bias_kernel(rpb_ref, o_ref, w_ref):
    p = pl.program_id(0)
    wq = lax.broadcasted_iota(jnp.int32, (GRID_W, PAIR_W), 0)
    lane = lax.broadcasted_iota(jnp.int32, (GRID_W, PAIR_W), 1)
    wk = jnp.bitwise_and(lane, GRID_W - 1)
    diff = wk - wq
    cs = jnp.clip(wq - WIN_COLS // 2, 0, GRID_W - WIN_COLS)
    valid = jnp.logical_and(wk >= cs, wk < cs + WIN_COLS)
    n_dr = 2 * WIN_ROWS - 1
    n_dc = 2 * WIN_COLS - 1
    for hh in range(2):
        base = (2 * p + hh) * (n_dr * n_dc)
        for dr in range(n_dr):
            acc = jnp.zeros((GRID_W, PAIR_W), jnp.float32)
            for v in range(n_dc):
                acc = jnp.where(diff == v - (WIN_COLS - 1), rpb_ref[base + dr * n_dc + v], acc)
            w_ref[dr] = jnp.where(valid, acc * LOG2_E, NEG_INF)
        for oi in range(WIN_ROWS):
            for jj in range(WIN_ROWS // 2):
                dr0 = 2 * jj - oi + WIN_ROWS - 1
                tile = jnp.where(lane < GRID_W, w_ref[dr0], w_ref[dr0 + 1])
                o_ref[0, oi, hh * GRID_W:(hh + 1) * GRID_W, jj * PAIR_W:(jj + 1) * PAIR_W] = tile


def _bias_table(rpb_l):
    band = WIN_ROWS * GRID_W
    est = 2 * _nbytes((WIN_ROWS, PAIR_W, band), jnp.float32) + _nbytes((16, GRID_W, PAIR_W), jnp.float32)
    return pl.pallas_call(
        _bias_kernel,
        grid=(N_PAIRS,),
        in_specs=[pl.BlockSpec(memory_space=pltpu.SMEM)],
        out_specs=pl.BlockSpec((1, WIN_ROWS, PAIR_W, band), lambda p: (p, 0, 0, 0)),
        out_shape=jax.ShapeDtypeStruct((N_PAIRS, WIN_ROWS, PAIR_W, band), jnp.float32),
        scratch_shapes=[pltpu.VMEM((2 * WIN_ROWS - 1, GRID_W, PAIR_W), jnp.float32)],
        compiler_params=_params(("parallel",), est, claim_all_vmem=True),
        name="bias_table",
    )(rpb_l.reshape(-1))


def _attn_kernel(q_ref, k_ref, v_ref, bias_ref, *refs, rows, n_next):
    o_ref, s_ref = refs[n_next], refs[-1]
    for src_ref, dst_ref in zip(refs[:n_next], refs[n_next + 1:-1]):
        dst_ref[...] = src_ref[...].astype(dst_ref.dtype)

    band = WIN_ROWS * GRID_W
    lo = lax.broadcasted_iota(jnp.int32, (GRID_W, PAIR_W), 1) < HEAD_DIM
    scale = HEAD_DIM ** -0.5 * LOG2_E
    n_groups = rows // ATTN_ROWS

    def offsets(g, i):
        r = g * ATTN_ROWS + i
        rs = jnp.clip(r - WIN_ROWS // 2, 0, rows - WIN_ROWS)
        return (r - rs, pl.multiple_of(r * GRID_W, GRID_W), pl.multiple_of(rs * GRID_W, GRID_W))

    def scores(g, slot):
        for i in range(ATTN_ROWS):
            oi, q0, k0 = offsets(g, i)
            q2 = q_ref[pl.ds(q0, GRID_W), :]
            zero = jnp.zeros_like(q2)
            qs = jnp.concatenate([jnp.where(lo, q2, zero), jnp.where(lo, zero, q2)], axis=0)
            s = lax.dot_general(qs, k_ref[pl.ds(k0, band), :], (((1,), (1,)), ((), ())),
                                preferred_element_type=jnp.float32)
            s_ref[slot, i] = s * scale + bias_ref[0, oi]

    def finish(g, slot):
        for i in range(ATTN_ROWS):
            _, q0, k0 = offsets(g, i)
            s = s_ref[slot, i]
            e = jnp.exp2(s - jnp.max(s, axis=-1, keepdims=True))
            den = jnp.sum(e, axis=-1, keepdims=True)
            vb = v_ref[pl.ds(k0, band), :]
            pv = jnp.dot(e.astype(vb.dtype), vb, preferred_element_type=jnp.float32) / den
            o = jnp.where(lo, pv[:GRID_W], pv[GRID_W:])
            o_ref[pl.ds(q0, GRID_W), :] = o.astype(o_ref.dtype)

    scores(0, 0)

    def body(gp, carry):
        g = 2 * gp
        scores(g + 1, 1)
        finish(g, 0)
        scores(jnp.minimum(g + 2, n_groups - 1), 0)
        finish(g + 1, 1)
        return carry

    lax.fori_loop(0, n_groups // 2, body, 0, unroll=4)


def _cast_block(shape, n_steps):
    r, c = shape
    for split in (1, 2, 4, 8):
        if (r * split) % n_steps == 0 and c % split == 0:
            br, bc = r * split // n_steps, c // split
            if br % (2 * SUBLANES) == 0 and bc % LANES == 0:
                return br, bc, split
    raise ValueError(f"no cast tiling for {shape} over {n_steps} steps")


def _attention(qkv, bias, batch, seq, next_weights, next_layer):
    m = qkv.shape[0]
    rows = seq // GRID_W
    assert rows >= WIN_ROWS and rows % (2 * ATTN_ROWS) == 0 and qkv.shape[1] == 3 * N_PAIRS * PAIR_W
    band = WIN_ROWS * GRID_W
    blk = lambda off: pl.BlockSpec((seq, PAIR_W), lambda b, p: (b, off + p))
    s_shape = (2, ATTN_ROWS, PAIR_W, band)
    est = 2 * (4 * _nbytes((seq, PAIR_W), jnp.bfloat16) + _nbytes((WIN_ROWS, PAIR_W, band), jnp.float32))
    est += 2 * _nbytes(s_shape, jnp.float32)
    cast_in_specs, cast_out_specs, cast_shapes = [], [], []
    for w in next_weights:
        br, bc, split = _cast_block(w.shape[1:], batch * N_PAIRS)
        step = lambda b, p: b * N_PAIRS + p
        cast_in_specs.append(pl.BlockSpec(
            (None, br, bc), lambda b, p, split=split: (next_layer, step(b, p) // split, step(b, p) % split)))
        cast_out_specs.append(pl.BlockSpec(
            (br, bc), lambda b, p, split=split: (step(b, p) // split, step(b, p) % split)))
        cast_shapes.append(jax.ShapeDtypeStruct(w.shape[1:], jnp.bfloat16))
        est += 2 * (_nbytes((br, bc), w.dtype) + _nbytes((br, bc), jnp.bfloat16))
    res = pl.pallas_call(
        functools.partial(_attn_kernel, rows=rows, n_next=len(next_weights)),
        grid=(batch, N_PAIRS),
        in_specs=[blk(0), blk(N_PAIRS), blk(2 * N_PAIRS),
                  pl.BlockSpec((1, WIN_ROWS, PAIR_W, band), lambda b, p: (p, 0, 0, 0))] + cast_in_specs,
        out_specs=[pl.BlockSpec((seq, PAIR_W), lambda b, p: (b, p))] + cast_out_specs,
        out_shape=[jax.ShapeDtypeStruct((m, N_PAIRS * PAIR_W), jnp.bfloat16)] + cast_shapes,
        scratch_shapes=[pltpu.VMEM(s_shape, jnp.float32)],
        compiler_params=_params(("parallel", "parallel"), est, claim_all_vmem=True),
        name="natten",
    )(qkv, qkv, qkv, bias, *next_weights)
    return res[0], tuple(res[1:])


def _outproj_kernel(yc_ref, ya_ref, wt_ref, wb_ref, x_ref, g1_ref, g2_ref, xo_ref, h_ref):
    sub = yc_ref.shape[0] // OUTPROJ_SUBBLOCKS
    for b in range(OUTPROJ_SUBBLOCKS):
        rows = pl.ds(b * sub, sub)
        mix = jnp.dot(yc_ref[rows, :], wt_ref[...], preferred_element_type=jnp.float32)
        mix = mix + jnp.dot(ya_ref[rows, :], wb_ref[...], preferred_element_type=jnp.float32)
        xn = x_ref[rows, :] + _rms(mix, g1_ref[...])
        xo_ref[rows, :] = xn
        h_ref[rows, :] = _rms(xn, g2_ref[...]).astype(h_ref.dtype)


def _outproj(yc, ya, w, x2, g_post, g_next, tm=512):
    m, d = x2.shape
    c = yc.shape[1]
    est = 2 * (2 * _nbytes((tm, c), yc.dtype) + 2 * _nbytes((c, d), w.dtype)
               + 2 * _nbytes((tm, d), jnp.float32) + _nbytes((tm, d), jnp.bfloat16))
    est += 2 * _nbytes((tm, d), jnp.float32)
    act = lambda width: pl.BlockSpec((tm, width), lambda i: (i, 0))
    wspec = lambda half: pl.BlockSpec((c, d), lambda i: (half, 0))
    vec = pl.BlockSpec((1, d), lambda i: (0, 0))
    return pl.pallas_call(
        _outproj_kernel,
        grid=(m // tm,),
        in_specs=[act(c), act(c), wspec(0), wspec(1), act(d), vec, vec],
        out_specs=[act(d), act(d)],
        out_shape=[jax.ShapeDtypeStruct((m, d), jnp.float32),
                   jax.ShapeDtypeStruct((m, d), jnp.bfloat16)],
        compiler_params=_params(("parallel",), est),
        name="outproj",
    )(yc, ya, w, w, x2, g_post, g_next)


def _mlp_kernel(h_ref, wu_ref, wd_ref, x_ref, g1_ref, g2_ref, xo_ref, *rest):
    f = pl.program_id(1)

    @pl.when(f == 0)
    def _():
        xo_ref[...] = jnp.zeros_like(xo_ref)

    hid = jnp.dot(h_ref[...], wu_ref[...], preferred_element_type=jnp.float32)
    hid = jnp.square(jnp.maximum(hid, 0.0)).astype(wd_ref.dtype)
    xo_ref[...] += jnp.dot(hid, wd_ref[...], preferred_element_type=jnp.float32)

    @pl.when(f == pl.num_programs(1) - 1)
    def _():
        xn = x_ref[...] + _rms(xo_ref[...], g1_ref[...])
        xo_ref[...] = xn
        if rest:
            rest[0][...] = _rms(xn, g2_ref[...]).astype(rest[0].dtype)


def _mlp(h, w_up, w_down, x2, g_post, g_next, emit_next, tm=512, tf=1024):
    m, d = x2.shape
    nf = w_up.shape[1] // tf
    est = 2 * (_nbytes((tm, d), h.dtype) + _nbytes((tm, d), jnp.float32)
               + _nbytes((d, tf), w_up.dtype) + _nbytes((tf, d), w_down.dtype)
               + _nbytes((tm, d), jnp.float32) + _nbytes((tm, d), jnp.bfloat16) * emit_next)
    est += 2 * _nbytes((tm, tf), jnp.float32) + _nbytes((tm, d), jnp.float32)
    act = pl.BlockSpec((tm, d), lambda i, f: (i, 0))
    vec = pl.BlockSpec((1, d), lambda i, f: (0, 0))
    in_specs = [act,
                pl.BlockSpec((d, tf), lambda i, f: (0, f)),
                pl.BlockSpec((tf, d), lambda i, f: (f, 0)),
                act, vec, vec]
    out_specs = [act]
    out_shape = [jax.ShapeDtypeStruct((m, d), jnp.float32)]
    if emit_next:
        out_specs.append(act)
        out_shape.append(jax.ShapeDtypeStruct((m, d), jnp.bfloat16))
    res = pl.pallas_call(
        _mlp_kernel,
        grid=(m // tm, nf),
        in_specs=in_specs,
        out_specs=out_specs,
        out_shape=out_shape,
        compiler_params=_params(("parallel", "arbitrary"), est),
        name="mlp",
    )(h, w_up, w_down, x2, g_post, g_next)
    return res[0], (res[1] if emit_next else None)


def kernel(x, w_in, w_dw, b_dw, conv_ln_g, conv_ln_b, rpb, w_out, w_up, w_down,
           pre_mix_g, post_mix_g, pre_mlp_g, post_mlp_g):
    batch, seq, d = x.shape
    depth = w_in.shape[0]
    conv_w = w_dw.shape[2]
    assert seq % GRID_W == 0 and w_in.shape[2] == 2 * conv_w + 3 * NA_HEADS * HEAD_DIM
    bf16 = jnp.bfloat16
    row = lambda v: v.reshape(1, -1)
    rows3 = lambda v: v.reshape(depth, 1, -1)

    b_dw3, ln_g3, ln_b3 = rows3(b_dw), rows3(conv_ln_g), rows3(conv_ln_b)
    qkv_w = 3 * NA_HEADS * HEAD_DIM

    mxu_weights = (w_in, w_out, w_up, w_down)
    w_in_b, w_out_b, w_up_b, w_down_b = (w[0].astype(bf16) for w in mxu_weights)

    x2 = x.reshape(batch * seq, d)
    h = _prenorm(x2, row(pre_mix_g[0]))
    for l in range(depth):
        last = l == depth - 1
        u = _glu_matmul(h, w_in_b, conv_w)
        qkv = _matmul(h, w_in_b, 2 * conv_w, qkv_w, bf16, name="inproj_qkv")
        yc = _conv_group(u, w_dw, b_dw3, ln_g3, ln_b3, l, seq)
        ya, nxt = _attention(qkv, _bias_table(rpb[l]), batch, seq, () if last else mxu_weights, l + 1)
        x2, h = _outproj(yc, ya, w_out_b, x2, row(post_mix_g[l]), row(pre_mlp_g[l]))
        g_next = row(post_mlp_g[l] if last else pre_mix_g[l + 1])
        x2, h = _mlp(h, w_up_b, w_down_b, x2, row(post_mlp_g[l]), g_next, emit_next=not last)
        if not last:
            w_in_b, w_out_b, w_up_b, w_down_b = nxt
    return x2.reshape(batch, seq, d)
```

```python
import functools

import jax
import jax.numpy as jnp
from jax import lax
from jax.experimental import pallas as pl
from jax.experimental.pallas import tpu as pltpu

LANES = 128
SUBLANES = 8
GRID_W = 64
NA_HEADS = 16
HEAD_DIM = 64
PAIR_W = 2 * HEAD_DIM
N_PAIRS = NA_HEADS // 2
CONV_K = 31
WIN_ROWS = 8
WIN_COLS = 16
RMS_EPS = 1e-6
LN_EPS = 1e-5
NEG_INF = -1e30
LOG2_E = 1.4426950408889634

CONV_HALO = 16
CONV_ROWS = 128
NORM_ROWS = 16
OUTPROJ_SUBBLOCKS = 2
ATTN_ROWS = 2

V7X_VMEM_BYTES = 64 * 1024 * 1024
VMEM_RESERVE = 6 * 1024 * 1024


def _params(semantics, est_bytes, claim_all_vmem=False):
    limit = V7X_VMEM_BYTES - VMEM_RESERVE
    if not claim_all_vmem:
        limit = min(int(est_bytes) + VMEM_RESERVE, limit)
    return pltpu.CompilerParams(dimension_semantics=semantics, vmem_limit_bytes=limit)


def _nbytes(shape, dtype):
    n = 1
    for s in shape:
        n *= s
    return n * jnp.dtype(dtype).itemsize


def _sigmoid(x):
    return 0.5 * jnp.tanh(0.5 * x) + 0.5


def _rms(xf, g):
    y = xf * lax.rsqrt(jnp.mean(xf * xf, axis=-1, keepdims=True) + RMS_EPS)
    return y * g


def _prenorm_kernel(x_ref, g_ref, h_ref):
    h_ref[...] = _rms(x_ref[...], g_ref[...]).astype(h_ref.dtype)


def _prenorm(x2, g, tm=512):
    m, d = x2.shape
    est = 2 * (_nbytes((tm, d), jnp.float32) + _nbytes((tm, d), jnp.bfloat16))
    return pl.pallas_call(
        _prenorm_kernel,
        grid=(m // tm,),
        in_specs=[pl.BlockSpec((tm, d), lambda i: (i, 0)),
                  pl.BlockSpec((1, d), lambda i: (0, 0))],
        out_specs=pl.BlockSpec((tm, d), lambda i: (i, 0)),
        out_shape=jax.ShapeDtypeStruct((m, d), jnp.bfloat16),
        compiler_params=_params(("parallel",), est),
        name="prenorm",
    )(x2, g)


def _matmul_kernel(h_ref, w_ref, o_ref):
    o_ref[...] = jnp.dot(h_ref[...], w_ref[...],
                         preferred_element_type=jnp.float32).astype(o_ref.dtype)


def _matmul(h, w, col0, n, out_dtype, tm=2048, tn=1024, name="inproj"):
    m, k = h.shape
    assert col0 % tn == 0 and n % tn == 0 and m % tm == 0
    est = 2 * (_nbytes((tm, k), h.dtype) + _nbytes((k, tn), w.dtype) + _nbytes((tm, tn), out_dtype))
    est += _nbytes((tm, tn), jnp.float32)
    return pl.pallas_call(
        _matmul_kernel,
        grid=(m // tm, n // tn),
        in_specs=[pl.BlockSpec((tm, k), lambda i, j: (i, 0)),
                  pl.BlockSpec((k, tn), lambda i, j: (0, col0 // tn + j))],
        out_specs=pl.BlockSpec((tm, tn), lambda i, j: (i, j)),
        out_shape=jax.ShapeDtypeStruct((m, n), out_dtype),
        compiler_params=_params(("parallel", "arbitrary"), est),
        name=name,
    )(h, w)


def _glu_matmul_kernel(h_ref, wa_ref, wg_ref, o_ref):
    h = h_ref[...]
    a = jnp.dot(h, wa_ref[...], preferred_element_type=jnp.float32)
    gate = jnp.dot(h, wg_ref[...], preferred_element_type=jnp.float32)
    o_ref[...] = a * _sigmoid(gate)


def _glu_matmul(h, w, c, tm=2048, tn=512):
    m, k = h.shape
    assert c % tn == 0 and m % tm == 0
    est = 2 * (_nbytes((tm, k), h.dtype) + 2 * _nbytes((k, tn), w.dtype) + _nbytes((tm, tn), jnp.float32))
    est += 4 * _nbytes((tm, tn), jnp.float32)
    return pl.pallas_call(
        _glu_matmul_kernel,
        grid=(m // tm, c // tn),
        in_specs=[pl.BlockSpec((tm, k), lambda i, j: (i, 0)),
                  pl.BlockSpec((k, tn), lambda i, j: (0, j)),
                  pl.BlockSpec((k, tn), lambda i, j: (0, c // tn + j))],
        out_specs=pl.BlockSpec((tm, tn), lambda i, j: (i, j)),
        out_shape=jax.ShapeDtypeStruct((m, c), jnp.float32),
        compiler_params=_params(("parallel", "arbitrary"), est),
        name="inproj_glu",
    )(h, w, w)


def _conv_kernel(u_ref, up_ref, un_ref, w_ref, b_ref, lg_ref, lb_ref,
                 o_ref, ext_ref, y_ref, *, tt, tiles_per_seq):
    ti = lax.rem(pl.program_id(0), tiles_per_seq)
    groups = ext_ref.shape[0]

    def put(row0, nrows, val):
        for gi in range(groups):
            ext_ref[gi, pl.ds(row0, nrows), :] = val[:, gi * LANES:(gi + 1) * LANES]

    put(CONV_HALO, tt, u_ref[...])
    up = up_ref[...]
    put(0, CONV_HALO, jnp.where(ti > 0, up, jnp.zeros_like(up)))
    un = un_ref[...]
    put(CONV_HALO + tt, CONV_HALO, jnp.where(ti < tiles_per_seq - 1, un, jnp.zeros_like(un)))

    first_tap = CONV_HALO - CONV_K // 2
    sub_tiles = CONV_ROWS // SUBLANES

    for gi in range(groups):
        cols = slice(gi * LANES, (gi + 1) * LANES)
        taps = [jnp.broadcast_to(w_ref[k:k + 1, cols], (SUBLANES, LANES)) for k in range(CONV_K)]

        def conv_chunk(c, carry, gi=gi, cols=cols, taps=taps):
            r0 = pl.multiple_of(c * CONV_ROWS, CONV_ROWS)
            for j in range(sub_tiles):
                row = r0 + first_tap + j * SUBLANES
                acc = ext_ref[gi, pl.ds(row, SUBLANES), :] * taps[0]
                for k in range(1, CONV_K):
                    acc = acc + ext_ref[gi, pl.ds(row + k, SUBLANES), :] * taps[k]
                y_ref[pl.ds(r0 + j * SUBLANES, SUBLANES), cols] = acc
            return carry

        lax.fori_loop(0, tt // CONV_ROWS, conv_chunk, 0)

    def norm_chunk(c, carry):
        r0 = pl.multiple_of(c * NORM_ROWS, NORM_ROWS)
        y = y_ref[pl.ds(r0, NORM_ROWS), :] + b_ref[...]
        mu = jnp.mean(y, axis=-1, keepdims=True)
        yc = y - mu
        var = jnp.mean(yc * yc, axis=-1, keepdims=True)
        yn = yc * lax.rsqrt(var + LN_EPS) * lg_ref[...] + lb_ref[...]
        o_ref[pl.ds(r0, NORM_ROWS), :] = (yn * _sigmoid(yn)).astype(o_ref.dtype)
        return carry

    lax.fori_loop(0, tt // NORM_ROWS, norm_chunk, 0, unroll=8)


def _conv_group(u, w_dw, b_dw, ln_g, ln_b, layer, seq, tt=512):
    m, c = u.shape
    assert w_dw.shape[2] == c and seq % tt == 0 and tt % CONV_HALO == 0
    hb = tt // CONV_HALO
    last_hb = m // CONV_HALO - 1
    main = pl.BlockSpec((tt, c), lambda i: (i, 0))
    prev = pl.BlockSpec((CONV_HALO, c), lambda i: (jnp.maximum(i * hb - 1, 0), 0))
    nxt = pl.BlockSpec((CONV_HALO, c), lambda i: (jnp.minimum((i + 1) * hb, last_hb), 0))
    vec = pl.BlockSpec((None, 1, c), lambda i: (layer, 0, 0))
    est = 2 * (_nbytes((tt, c), u.dtype) + 2 * _nbytes((CONV_HALO, c), u.dtype)
               + _nbytes((tt, c), jnp.bfloat16)) + 3 * _nbytes((tt + 2 * CONV_HALO, c), jnp.float32)
    return pl.pallas_call(
        functools.partial(_conv_kernel, tt=tt, tiles_per_seq=seq // tt),
        grid=(m // tt,),
        in_specs=[main, prev, nxt,
                  pl.BlockSpec((None, CONV_K, c), lambda i: (layer, 0, 0)), vec, vec, vec],
        out_specs=pl.BlockSpec((tt, c), lambda i: (i, 0)),
        out_shape=jax.ShapeDtypeStruct((m, c), jnp.bfloat16),
        scratch_shapes=[pltpu.VMEM((c // LANES, tt + 2 * CONV_HALO, LANES), jnp.float32),
                        pltpu.VMEM((tt, c), jnp.float32)],
        compiler_params=_params(("parallel",), est, claim_all_vmem=True),
        name="conv_group",
    )(u, u, u, w_dw, b_dw, ln_g, ln_b)


def _bias_kernel(rpb_ref, o_ref, w_ref):
    p = pl.program_id(0)
    wq = lax.broadcasted_iota(jnp.int32, (GRID_W, PAIR_W), 0)
    lane = lax.broadcasted_iota(jnp.int32, (GRID_W, PAIR_W), 1)
    wk = jnp.bitwise_and(lane, GRID_W - 1)
    diff = wk - wq
    cs = jnp.clip(wq - WIN_COLS // 2, 0, GRID_W - WIN_COLS)
    valid = jnp.logical_and(wk >= cs, wk < cs + WIN_COLS)
    n_dr = 2 * WIN_ROWS - 1
    n_dc = 2 * WIN_COLS - 1
    for hh in range(2):
        base = (2 * p + hh) * (n_dr * n_dc)
        for dr in range(n_dr):
            acc = jnp.zeros((GRID_W, PAIR_W), jnp.float32)
            for v in range(n_dc):
                acc = jnp.where(diff == v - (WIN_COLS - 1), rpb_ref[base + dr * n_dc + v], acc)
            w_ref[dr] = jnp.where(valid, acc * LOG2_E, NEG_INF)
        for oi in range(WIN_ROWS):
            for jj in range(WIN_ROWS // 2):
                dr0 = 2 * jj - oi + WIN_ROWS - 1
                tile = jnp.where(lane < GRID_W, w_ref[dr0], w_ref[dr0 + 1])
                o_ref[0, oi, hh * GRID_W:(hh + 1) * GRID_W, jj * PAIR_W:(jj + 1) * PAIR_W] = tile


def _bias_table(rpb_l):
    band = WIN_ROWS * GRID_W
    est = 2 * _nbytes((WIN_ROWS, PAIR_W, band), jnp.float32) + _nbytes((16, GRID_W, PAIR_W), jnp.float32)
    return pl.pallas_call(
        _bias_kernel,
        grid=(N_PAIRS,),
        in_specs=[pl.BlockSpec(memory_space=pltpu.SMEM)],
        out_specs=pl.BlockSpec((1, WIN_ROWS, PAIR_W, band), lambda p: (p, 0, 0, 0)),
        out_shape=jax.ShapeDtypeStruct((N_PAIRS, WIN_ROWS, PAIR_W, band), jnp.float32),
        scratch_shapes=[pltpu.VMEM((2 * WIN_ROWS - 1, GRID_W, PAIR_W), jnp.float32)],
        compiler_params=_params(("parallel",), est, claim_all_vmem=True),
        name="bias_table",
    )(rpb_l.reshape(-1))


def _attn_kernel(q_ref, k_ref, v_ref, bias_ref, *refs, rows, n_next):
    o_ref, s_ref = refs[n_next], refs[-1]
    for src_ref, dst_ref in zip(refs[:n_next], refs[n_next + 1:-1]):
        dst_ref[...] = src_ref[...].astype(dst_ref.dtype)

    band = WIN_ROWS * GRID_W
    lo = lax.broadcasted_iota(jnp.int32, (GRID_W, PAIR_W), 1) < HEAD_DIM
    scale = HEAD_DIM ** -0.5 * LOG2_E
    n_groups = rows // ATTN_ROWS

    def offsets(g, i):
        r = g * ATTN_ROWS + i
        rs = jnp.clip(r - WIN_ROWS // 2, 0, rows - WIN_ROWS)
        return (r - rs, pl.multiple_of(r * GRID_W, GRID_W), pl.multiple_of(rs * GRID_W, GRID_W))

    def scores(g, slot):
        for i in range(ATTN_ROWS):
            oi, q0, k0 = offsets(g, i)
            q2 = q_ref[pl.ds(q0, GRID_W), :]
            zero = jnp.zeros_like(q2)
            qs = jnp.concatenate([jnp.where(lo, q2, zero), jnp.where(lo, zero, q2)], axis=0)
            s = lax.dot_general(qs, k_ref[pl.ds(k0, band), :], (((1,), (1,)), ((), ())),
                                preferred_element_type=jnp.float32)
            t = bias_ref[0, oi]
            s_ref[slot, i] = jnp.where(t > 0.5 * NEG_INF, s * scale + t, NEG_INF)

    def finish(g, slot):
        for i in range(ATTN_ROWS):
            _, q0, k0 = offsets(g, i)
            s = s_ref[slot, i]
            e = jnp.exp2(s - jnp.max(s, axis=-1, keepdims=True))
            den = jnp.sum(e, axis=-1, keepdims=True)
            vb = v_ref[pl.ds(k0, band), :]
            pv = jnp.dot(e.astype(vb.dtype), vb, preferred_element_type=jnp.float32) / den
            o = jnp.where(lo, pv[:GRID_W], pv[GRID_W:])
            o_ref[pl.ds(q0, GRID_W), :] = o.astype(o_ref.dtype)

    scores(0, 0)

    def body(gp, carry):
        g = 2 * gp
        scores(g + 1, 1)
        finish(g, 0)
        scores(jnp.minimum(g + 2, n_groups - 1), 0)
        finish(g + 1, 1)
        return carry

    lax.fori_loop(0, n_groups // 2, body, 0, unroll=4)


def _cast_block(shape, n_steps):
    r, c = shape
    for split in (1, 2, 4, 8):
        if (r * split) % n_steps == 0 and c % split == 0:
            br, bc = r * split // n_steps, c // split
            if br % (2 * SUBLANES) == 0 and bc % LANES == 0:
                return br, bc, split
    raise ValueError(f"no cast tiling for {shape} over {n_steps} steps")


def _attention(qkv, bias, batch, seq, next_weights, next_layer):
    m = qkv.shape[0]
    rows = seq // GRID_W
    assert rows >= WIN_ROWS and rows % (2 * ATTN_ROWS) == 0 and qkv.shape[1] == 3 * N_PAIRS * PAIR_W
    band = WIN_ROWS * GRID_W
    blk = lambda off: pl.BlockSpec((seq, PAIR_W), lambda b, p: (b, off + p))
    s_shape = (2, ATTN_ROWS, PAIR_W, band)
    est = 2 * (4 * _nbytes((seq, PAIR_W), jnp.bfloat16) + _nbytes((WIN_ROWS, PAIR_W, band), jnp.float32))
    est += 2 * _nbytes(s_shape, jnp.float32)
    cast_in_specs, cast_out_specs, cast_shapes = [], [], []
    for w in next_weights:
        br, bc, split = _cast_block(w.shape[1:], batch * N_PAIRS)
        step = lambda b, p: b * N_PAIRS + p
        cast_in_specs.append(pl.BlockSpec(
            (None, br, bc), lambda b, p, split=split: (next_layer, step(b, p) // split, step(b, p) % split)))
        cast_out_specs.append(pl.BlockSpec(
            (br, bc), lambda b, p, split=split: (step(b, p) // split, step(b, p) % split)))
        cast_shapes.append(jax.ShapeDtypeStruct(w.shape[1:], jnp.bfloat16))
        est += 2 * (_nbytes((br, bc), w.dtype) + _nbytes((br, bc), jnp.bfloat16))
    res = pl.pallas_call(
        functools.partial(_attn_kernel, rows=rows, n_next=len(next_weights)),
        grid=(batch, N_PAIRS),
        in_specs=[blk(0), blk(N_PAIRS), blk(2 * N_PAIRS),
                  pl.BlockSpec((1, WIN_ROWS, PAIR_W, band), lambda b, p: (p, 0, 0, 0))] + cast_in_specs,
        out_specs=[pl.BlockSpec((seq, PAIR_W), lambda b, p: (b, p))] + cast_out_specs,
        out_shape=[jax.ShapeDtypeStruct((m, N_PAIRS * PAIR_W), jnp.bfloat16)] + cast_shapes,
        scratch_shapes=[pltpu.VMEM(s_shape, jnp.float32)],
        compiler_params=_params(("parallel", "parallel"), est, claim_all_vmem=True),
        name="natten",
    )(qkv, qkv, qkv, bias, *next_weights)
    return res[0], tuple(res[1:])


def _outproj_kernel(yc_ref, ya_ref, wt_ref, wb_ref, x_ref, g1_ref, g2_ref, xo_ref, h_ref):
    sub = yc_ref.shape[0] // OUTPROJ_SUBBLOCKS
    for b in range(OUTPROJ_SUBBLOCKS):
        rows = pl.ds(b * sub, sub)
        mix = jnp.dot(yc_ref[rows, :], wt_ref[...], preferred_element_type=jnp.float32)
        mix = mix + jnp.dot(ya_ref[rows, :], wb_ref[...], preferred_element_type=jnp.float32)
        xn = x_ref[rows, :] + _rms(mix, g1_ref[...])
        xo_ref[rows, :] = xn
        h_ref[rows, :] = _rms(xn, g2_ref[...]).astype(h_ref.dtype)


def _outproj(yc, ya, w, x2, g_post, g_next, tm=512):
    m, d = x2.shape
    c = yc.shape[1]
    est = 2 * (2 * _nbytes((tm, c), yc.dtype) + 2 * _nbytes((c, d), w.dtype)
               + 2 * _nbytes((tm, d), jnp.float32) + _nbytes((tm, d), jnp.bfloat16))
    est += 2 * _nbytes((tm, d), jnp.float32)
    act = lambda width: pl.BlockSpec((tm, width), lambda i: (i, 0))
    wspec = lambda half: pl.BlockSpec((c, d), lambda i: (half, 0))
    vec = pl.BlockSpec((1, d), lambda i: (0, 0))
    return pl.pallas_call(
        _outproj_kernel,
        grid=(m // tm,),
        in_specs=[act(c), act(c), wspec(0), wspec(1), act(d), vec, vec],
        out_specs=[act(d), act(d)],
        out_shape=[jax.ShapeDtypeStruct((m, d), jnp.float32),
                   jax.ShapeDtypeStruct((m, d), jnp.bfloat16)],
        compiler_params=_params(("parallel",), est),
        name="outproj",
    )(yc, ya, w, w, x2, g_post, g_next)


def _mlp_kernel(h_ref, wu_ref, wd_ref, x_ref, g1_ref, g2_ref, xo_ref, *rest):
    f = pl.program_id(1)

    @pl.when(f == 0)
    def _():
        xo_ref[...] = jnp.zeros_like(xo_ref)

    hid = jnp.dot(h_ref[...], wu_ref[...], preferred_element_type=jnp.float32)
    hid = jnp.square(jnp.maximum(hid, 0.0)).astype(wd_ref.dtype)
    xo_ref[...] += jnp.dot(hid, wd_ref[...], preferred_element_type=jnp.float32)

    @pl.when(f == pl.num_programs(1) - 1)
    def _():
        xn = x_ref[...] + _rms(xo_ref[...], g1_ref[...])
        xo_ref[...] = xn
        if rest:
            rest[0][...] = _rms(xn, g2_ref[...]).astype(rest[0].dtype)


def _mlp(h, w_up, w_down, x2, g_post, g_next, emit_next, tm=512, tf=1024):
    m, d = x2.shape
    nf = w_up.shape[1] // tf
    est = 2 * (_nbytes((tm, d), h.dtype) + _nbytes((tm, d), jnp.float32)
               + _nbytes((d, tf), w_up.dtype) + _nbytes((tf, d), w_down.dtype)
               + _nbytes((tm, d), jnp.float32) + _nbytes((tm, d), jnp.bfloat16) * emit_next)
    est += 2 * _nbytes((tm, tf), jnp.float32) + _nbytes((tm, d), jnp.float32)
    act = pl.BlockSpec((tm, d), lambda i, f: (i, 0))
    vec = pl.BlockSpec((1, d), lambda i, f: (0, 0))
    in_specs = [act,
                pl.BlockSpec((d, tf), lambda i, f: (0, f)),
                pl.BlockSpec((tf, d), lambda i, f: (f, 0)),
                act, vec, vec]
    out_specs = [act]
    out_shape = [jax.ShapeDtypeStruct((m, d), jnp.float32)]
    if emit_next:
        out_specs.append(act)
        out_shape.append(jax.ShapeDtypeStruct((m, d), jnp.bfloat16))
    res = pl.pallas_call(
        _mlp_kernel,
        grid=(m // tm, nf),
        in_specs=in_specs,
        out_specs=out_specs,
        out_shape=out_shape,
        compiler_params=_params(("parallel", "arbitrary"), est),
        name="mlp",
    )(h, w_up, w_down, x2, g_post, g_next)
    return res[0], (res[1] if emit_next else None)


def kernel(x, w_in, w_dw, b_dw, conv_ln_g, conv_ln_b, rpb, w_out, w_up, w_down,
           pre_mix_g, post_mix_g, pre_mlp_g, post_mlp_g):
    batch, seq, d = x.shape
    depth = w_in.shape[0]
    conv_w = w_dw.shape[2]
    assert seq % GRID_W == 0 and w_in.shape[2] == 2 * conv_w + 3 * NA_HEADS * HEAD_DIM
    bf16 = jnp.bfloat16
    row = lambda v: v.reshape(1, -1)
    rows3 = lambda v: v.reshape(depth, 1, -1)

    b_dw3, ln_g3, ln_b3 = rows3(b_dw), rows3(conv_ln_g), rows3(conv_ln_b)
    qkv_w = 3 * NA_HEADS * HEAD_DIM

    mxu_weights = (w_in, w_out, w_up, w_down)
    w_in_b, w_out_b, w_up_b, w_down_b = (w[0].astype(bf16) for w in mxu_weights)

    x2 = x.reshape(batch * seq, d)
    h = _prenorm(x2, row(pre_mix_g[0]))
    for l in range(depth):
        last = l == depth - 1
        u = _glu_matmul(h, w_in_b, conv_w)
        qkv = _matmul(h, w_in_b, 2 * conv_w, qkv_w, bf16, name="inproj_qkv")
        yc = _conv_group(u, w_dw, b_dw3, ln_g3, ln_b3, l, seq)
        ya, nxt = _attention(qkv, _bias_table(rpb[l]), batch, seq, () if last else mxu_weights, l + 1)
        x2, h = _outproj(yc, ya, w_out_b, x2, row(post_mix_g[l]), row(pre_mlp_g[l]))
        g_next = row(post_mlp_g[l] if last else pre_mix_g[l + 1])
        x2, h = _mlp(h, w_up_b, w_down_b, x2, row(post_mlp_g[l]), g_next, emit_next=not last)
        if not last:
            w_in_b, w_out_b, w_up_b, w_down_b = nxt
    return x2.reshape(batch, seq, d)
```
